```python
import math
import jax, jax.numpy as jnp
from jax import lax
import numpy as np

D_MODEL = 1024
BATCH = 16
SEQ = 4096
DEPTH = 4
DEC_BATCH = 8
DEC_SEQ = 2048
PAST_LEN = 128

GRID_W = 64
CHUNK = 128
SSD_HEADS = 8
SSD_HEADDIM = 64
SSD_WIDTH = SSD_HEADS * SSD_HEADDIM
SSD_GROUPS = 2
SSD_HPG = SSD_HEADS // SSD_GROUPS
SSD_STATE = 128
SSD_BC = SSD_GROUPS * SSD_STATE
SSD_XBC = SSD_WIDTH + 2 * SSD_BC
SSD_CONV = 5
RET_HEADS = 4
RET_QK = 64
RET_V = 128
RET_WIDTH = RET_HEADS * RET_V
ROPE_BASE = 10000.0
NA_HEADS = 8
NA_HEADDIM = 64
NA_WIDTH = NA_HEADS * NA_HEADDIM
NA_ROWS = 8
NA_COLS = 16
N_BRANCH = 3
BRANCH_WIDTH = 512
D_FF = 2816
FFN_CONV = 3
EPS = 1e-6
PROJ_SIZES = (SSD_WIDTH,
              SSD_XBC,
              2 * SSD_HEADS,
              RET_HEADS * RET_QK,
              RET_HEADS * RET_QK,
              RET_WIDTH,
              RET_WIDTH,
              NA_WIDTH,
              NA_WIDTH,
              NA_WIDTH,
              N_BRANCH * D_MODEL)
PROJ_TOTAL = sum(PROJ_SIZES)

kernel_name = "hybrid_ssd_retention_natten_encoder"


def rmsnorm(x, g=None):
    xf = x.astype(jnp.float32)
    y = xf * lax.rsqrt(jnp.mean(xf * xf, axis=-1, keepdims=True) + EPS)
    if g is not None:
        y = y * g.astype(jnp.float32)
    return y.astype(x.dtype)


def dwconv(x, w, bias):
    k = w.shape[0]
    out = lax.conv_general_dilated(
        x, w[:, None, :].astype(x.dtype), window_strides=(1,),
        padding=[(k // 2, k // 2)], dimension_numbers=('NWC', 'WIO', 'NWC'),
        feature_group_count=x.shape[-1])
    return out + bias.astype(x.dtype)


def rotary(x, pos):
    half = x.shape[-1] // 2
    inv = 1.0 / (ROPE_BASE ** (jnp.arange(half, dtype=jnp.float32) / half))
    ang = pos[:, None] * inv[None, :]
    cos = jnp.cos(ang)[None, :, None, :]
    sin = jnp.sin(ang)[None, :, None, :]
    xf = x.astype(jnp.float32)
    x1, x2 = xf[..., :half], xf[..., half:]
    return jnp.concatenate([x1 * cos - x2 * sin, x2 * cos + x1 * sin], axis=-1).astype(x.dtype)


def decay_scan(q, k, v, log_a, strict):
    b, l, g, n = q.shape
    r, p = v.shape[3], v.shape[4]
    c = l // CHUNK
    qc = q.reshape(b, c, CHUNK, g, n)
    kc = k.reshape(b, c, CHUNK, g, n)
    vc = v.reshape(b, c, CHUNK, g, r, p)
    acum = jnp.cumsum(log_a.astype(jnp.float32).reshape(b, c, CHUNK, g, r), axis=2)
    at = jnp.moveaxis(acum, 2, -1)
    seg = at[..., :, None] - at[..., None, :]
    mask = jnp.tril(jnp.ones((CHUNK, CHUNK), dtype=bool), -1 if strict else 0)
    decay = jnp.exp(jnp.where(mask, seg, -jnp.inf))
    scores = jnp.einsum('bcign,bcjgn->bcgij', qc, kc).astype(jnp.float32)
    y_intra = jnp.einsum('bcgrij,bcjgrp->bcigrp', scores[:, :, :, None] * decay, vc)
    last = at[..., -1:]
    states = jnp.einsum('bcjgn,bcgrj,bcjgrp->bcgrnp', kc, jnp.exp(last - at), vc)
    chunk_decay = jnp.exp(last[..., 0])

    def step(s, inp):
        st, dec = inp
        return dec[..., None, None] * s + st, s

    s0 = jnp.zeros((b, g, r, n, p), states.dtype)
    _, prev = lax.scan(step, s0, (jnp.moveaxis(states, 1, 0), jnp.moveaxis(chunk_decay, 1, 0)))
    prev = jnp.moveaxis(prev, 0, 1)
    y_inter = jnp.einsum('bcign,bcgrnp,bcgri->bcigrp', qc, prev, jnp.exp(at))
    return (y_intra + y_inter).reshape(b, l, g, r, p).astype(v.dtype)


def bidir_scan(q, k, v_f, la_f, v_b, la_b):
    flip = lambda t: jnp.flip(t, axis=1)
    y_f = decay_scan(q, k, v_f, la_f, strict=False)
    y_b = flip(decay_scan(flip(q), flip(k), flip(v_b), flip(la_b), strict=True))
    return y_f + y_b


def neighborhood_attention(q, k, v, rpb):
    b, l, h, d = q.shape
    rows = l // GRID_W
    kh = min(NA_ROWS, rows)
    qg = q.reshape(b, rows, GRID_W, h, d)
    kg = k.reshape(b, rows, GRID_W, h, d)
    vg = v.reshape(b, rows, GRID_W, h, d)
    cols = jnp.arange(GRID_W)
    col_start = jnp.clip(cols - NA_COLS // 2, 0, GRID_W - NA_COLS)
    col_idx = col_start[:, None] + jnp.arange(NA_COLS)[None, :]
    col_off = col_idx - cols[:, None]
    bias_col = rpb.astype(jnp.float32)[:, :, col_off + NA_COLS - 1]
    row_start = jnp.clip(jnp.arange(rows) - kh // 2, 0, rows - kh)
    scale = d ** -0.5

    def one_row(r):
        rs = row_start[r]
        kb = lax.dynamic_slice_in_dim(kg, rs, kh, axis=1)[:, :, col_idx]
        vb = lax.dynamic_slice_in_dim(vg, rs, kh, axis=1)[:, :, col_idx]
        qr = lax.dynamic_index_in_dim(qg, r, axis=1, keepdims=False)
        s = jnp.einsum('bqhd,bxqyhd->bhqxy', qr, kb).astype(jnp.float32) * scale
        row_off = rs + jnp.arange(kh) - r
        bias = jnp.take(bias_col, row_off + NA_ROWS - 1, axis=1)
        s = s + jnp.transpose(bias, (0, 2, 1, 3))[None]
        pr = jax.nn.softmax(s.reshape(b, h, GRID_W, kh * NA_COLS), axis=-1)
        pr = pr.reshape(b, h, GRID_W, kh, NA_COLS).astype(v.dtype)
        return jnp.einsum('bhqxy,bxqyhd->bqhd', pr, vb)

    out = lax.map(one_row, jnp.arange(rows))
    return jnp.moveaxis(out, 0, 1).reshape(b, l, h, d)


def encoder_layer(x, norm_mix, w_in, gate_bias, ssd_conv_w, ssd_conv_b, ssd_dt_bias,
                  ssd_a_log, ssd_d, ssd_norm, ret_theta, na_rpb, w_branch, w_out,
                  norm_ffn, ffn_w_up, ffn_conv_w, ffn_conv_b, ffn_w_down):
    b, l, _ = x.shape
    h = rmsnorm(x, norm_mix)
    proj = h @ w_in
    splits, acc = [], 0
    for s in PROJ_SIZES[:-1]:
        acc += s
        splits.append(acc)
    z, xbc, dt_raw, rq, rk, rv, rg, nq, nk, nv, gates = jnp.split(proj, splits, axis=-1)

    xbc = jax.nn.silu(dwconv(xbc, ssd_conv_w, ssd_conv_b))
    xs, bm, cm = jnp.split(xbc, [SSD_WIDTH, SSD_WIDTH + SSD_BC], axis=-1)
    xh = xs.reshape(b, l, SSD_GROUPS, SSD_HPG, SSD_HEADDIM)
    bm = bm.reshape(b, l, SSD_GROUPS, SSD_STATE)
    cm = cm.reshape(b, l, SSD_GROUPS, SSD_STATE)
    dt = jax.nn.softplus(dt_raw.astype(jnp.float32).reshape(b, l, 2, SSD_GROUPS, SSD_HPG)
                         + ssd_dt_bias.astype(jnp.float32).reshape(2, SSD_GROUPS, SSD_HPG))
    a = -jnp.exp(ssd_a_log.astype(jnp.float32)).reshape(2, SSD_GROUPS, SSD_HPG)
    dt_f, dt_b = dt[:, :, 0], dt[:, :, 1]
    y = bidir_scan(cm, bm, xh * dt_f[..., None], dt_f * a[0], xh * dt_b[..., None], dt_b * a[1])
    y = y + ssd_d.reshape(SSD_GROUPS, SSD_HPG)[..., None] * xh
    o_ssd = rmsnorm(y.astype(x.dtype).reshape(b, l, SSD_WIDTH) * jax.nn.silu(z), ssd_norm)

    pos = jnp.arange(l, dtype=jnp.float32)
    q = rotary(rq.reshape(b, l, RET_HEADS, RET_QK), pos)
    k = rotary(rk.reshape(b, l, RET_HEADS, RET_QK), pos) * (RET_QK ** -0.5)
    v = rv.reshape(b, l, RET_HEADS, 1, RET_V)
    log_gamma = -jnp.exp(ret_theta.astype(jnp.float32))
    la_f = jnp.broadcast_to(log_gamma[0][:, None], (b, l, RET_HEADS, 1))
    la_b = jnp.broadcast_to(log_gamma[1][:, None], (b, l, RET_HEADS, 1))
    yr = bidir_scan(q, k, v, la_f, v, la_b).reshape(b, l, RET_HEADS, RET_V)
    o_ret = rmsnorm(yr).reshape(b, l, RET_WIDTH).astype(x.dtype) * jax.nn.silu(rg)

    o_na = neighborhood_attention(nq.reshape(b, l, NA_HEADS, NA_HEADDIM),
                                  nk.reshape(b, l, NA_HEADS, NA_HEADDIM),
                                  nv.reshape(b, l, NA_HEADS, NA_HEADDIM),
                                  na_rpb).reshape(b, l, NA_WIDTH)

    gates = jax.nn.sigmoid((gates + gate_bias).astype(jnp.float32)).astype(x.dtype)
    gates = gates.reshape(b, l, N_BRANCH, D_MODEL)
    branches = (o_ssd, o_ret, o_na)
    merged = gates[:, :, 0] * (branches[0] @ w_branch[0])
    for i in range(1, N_BRANCH):
        merged = merged + gates[:, :, i] * (branches[i] @ w_branch[i])
    x = x + merged @ w_out

    h2 = rmsnorm(x, norm_ffn)
    u = dwconv(h2 @ ffn_w_up, ffn_conv_w, ffn_conv_b)
    val, gt = jnp.split(u, 2, axis=-1)
    x = x + (jax.nn.silu(gt) * val) @ ffn_w_down
    return x


def trunk(x, norm_mix, w_in, gate_bias, ssd_conv_w, ssd_conv_b, ssd_dt_bias, ssd_a_log,
          ssd_d, ssd_norm, ret_theta, na_rpb, w_branch, w_out, norm_ffn, ffn_w_up,
          ffn_conv_w, ffn_conv_b, ffn_w_down, norm_final):
    for i in range(DEPTH):
        x = encoder_layer(x, norm_mix[i], w_in[i], gate_bias[i], ssd_conv_w[i], ssd_conv_b[i],
                          ssd_dt_bias[i], ssd_a_log[i], ssd_d[i], ssd_norm[i], ret_theta[i],
                          na_rpb[i], w_branch[i], w_out[i], norm_ffn[i], ffn_w_up[i],
                          ffn_conv_w[i], ffn_conv_b[i], ffn_w_down[i])
    return rmsnorm(x, norm_final)


def setup_inputs(seed: int = 0) -> dict:
    key = jax.random.key(seed)
    ks = jax.random.split(key, 24)
    f32 = jnp.float32
    nrm = lambda k, shape, s: jax.random.normal(k, shape, f32) * s
    dt0 = jnp.exp(jax.random.uniform(ks[8], (DEPTH, 2, SSD_HEADS), f32,
                                     math.log(1e-3), math.log(1e-1)))
    ret_base = jnp.asarray(np.log(-np.log(1.0 - 2.0 ** (-5.0 - np.arange(RET_HEADS)))), dtype=f32)
    return {
        "x_prompt": jax.random.normal(ks[0], (BATCH, SEQ, D_MODEL), f32),
        "x_sample": jax.random.normal(ks[1], (DEC_BATCH, DEC_SEQ, D_MODEL), f32),
        "norm_mix": 1.0 + nrm(ks[2], (DEPTH, D_MODEL), 0.02),
        "w_in": nrm(ks[3], (DEPTH, D_MODEL, PROJ_TOTAL), D_MODEL ** -0.5),
        "gate_bias": nrm(ks[4], (DEPTH, N_BRANCH * D_MODEL), 0.01),
        "ssd_conv_w": nrm(ks[5], (DEPTH, SSD_CONV, SSD_XBC), SSD_CONV ** -0.5),
        "ssd_conv_b": nrm(ks[6], (DEPTH, SSD_XBC), 0.01),
        "ssd_dt_bias": dt0 + jnp.log(-jnp.expm1(-dt0)),
        "ssd_a_log": jnp.log(jax.random.uniform(ks[9], (DEPTH, 2, SSD_HEADS), f32, 1.0, 16.0)),
        "ssd_d": 1.0 + nrm(ks[10], (DEPTH, SSD_HEADS), 0.1),
        "ssd_norm": 1.0 + nrm(ks[11], (DEPTH, SSD_WIDTH), 0.02),
        "ret_theta": ret_base[None, None, :] + nrm(ks[12], (DEPTH, 2, RET_HEADS), 0.05),
        "na_rpb": nrm(ks[13], (DEPTH, NA_HEADS, 2 * NA_ROWS - 1, 2 * NA_COLS - 1), 0.02),
        "w_branch": nrm(ks[14], (DEPTH, N_BRANCH, BRANCH_WIDTH, D_MODEL), BRANCH_WIDTH ** -0.5),
        "w_out": nrm(ks[15], (DEPTH, D_MODEL, D_MODEL), D_MODEL ** -0.5),
        "norm_ffn": 1.0 + nrm(ks[16], (DEPTH, D_MODEL), 0.02),
        "ffn_w_up": nrm(ks[17], (DEPTH, D_MODEL, 2 * D_FF), D_MODEL ** -0.5),
        "ffn_conv_w": nrm(ks[18], (DEPTH, FFN_CONV, 2 * D_FF), FFN_CONV ** -0.5),
        "ffn_conv_b": nrm(ks[19], (DEPTH, 2 * D_FF), 0.01),
        "ffn_w_down": nrm(ks[20], (DEPTH, D_FF, D_MODEL), D_FF ** -0.5),
        "norm_final": 1.0 + nrm(ks[21], (D_MODEL,), 0.02),
    }


def reference(x_prompt, x_sample, norm_mix, w_in, gate_bias, ssd_conv_w, ssd_conv_b,
              ssd_dt_bias, ssd_a_log, ssd_d, ssd_norm, ret_theta, na_rpb, w_branch, w_out,
              norm_ffn, ffn_w_up, ffn_conv_w, ffn_conv_b, ffn_w_down, norm_final):
    y_prompt = trunk(x_prompt, norm_mix, w_in, gate_bias, ssd_conv_w, ssd_conv_b, ssd_dt_bias,
                     ssd_a_log, ssd_d, ssd_norm, ret_theta, na_rpb, w_branch, w_out, norm_ffn,
                     ffn_w_up, ffn_conv_w, ffn_conv_b, ffn_w_down, norm_final)
    y_sample = trunk(x_sample, norm_mix, w_in, gate_bias, ssd_conv_w, ssd_conv_b, ssd_dt_bias,
                     ssd_a_log, ssd_d, ssd_norm, ret_theta, na_rpb, w_branch, w_out, norm_ffn,
                     ffn_w_up, ffn_conv_w, ffn_conv_b, ffn_w_down, norm_final)
    return (y_prompt, y_sample)
```

```python
import functools

import jax
import jax.numpy as jnp
import numpy as np
from jax import lax
from jax.experimental import pallas as pl
from jax.experimental.pallas import tpu as pltpu

F32 = jnp.float32
BF16 = jnp.bfloat16

D_MODEL = 1024
GRID_W = 64
CHUNK = 128
SSD_HEADS = 8
SSD_HEADDIM = 64
SSD_WIDTH = SSD_HEADS * SSD_HEADDIM
SSD_GROUPS = 2
SSD_STATE = 128
SSD_BC = SSD_GROUPS * SSD_STATE
SSD_XBC = SSD_WIDTH + 2 * SSD_BC
SSD_CONV = 5
RET_HEADS = 4
RET_QK = 64
RET_V = 128
RET_WIDTH = RET_HEADS * RET_V
ROPE_BASE = 10000.0
NA_HEADS = 8
NA_HEADDIM = 64
NA_WIDTH = NA_HEADS * NA_HEADDIM
NA_ROWS = 8
NA_COLS = 16
N_BRANCH = 3
D_FF = 2816
FFN_CONV = 3
EPS = 1e-6

GATE_W = N_BRANCH * D_MODEL
SSD_IN_W = SSD_WIDTH + SSD_XBC
RET_QK_W = RET_HEADS * RET_QK
RET_IN_W = 2 * RET_QK_W + 2 * RET_WIDTH
NA_IN_W = 3 * NA_WIDTH
NA_WIN = NA_ROWS * GRID_W
HALO = 16
FF_CHUNK = 256
N_FF_CHUNKS = D_FF // FF_CHUNK
NEG_BIG = -1e30
VMEM_LIMIT = 56 * 1024 * 1024


def _sigmoid(v):
    return 1.0 / (1.0 + jnp.exp(-v))


def _silu(v):
    return v * _sigmoid(v)


def _dot(a, b):
    return jnp.dot(a, b, preferred_element_type=F32)


def _dot_nt(a, b):
    return lax.dot_general(a, b, (((1,), (1,)), ((), ())), preferred_element_type=F32)


def _dot3(x, m01):
    x1 = x.astype(BF16)
    r1 = x - x1.astype(F32)
    x2 = r1.astype(BF16)
    x3 = (r1 - x2.astype(F32)).astype(BF16)
    return _dot(x1, m01) + _dot(x2, m01) + _dot(x3, m01)


def _tile_pos(i, tm, segs):
    t0 = i * tm
    pos = None
    length = None
    base = 0
    for n_seq, seq_len in segs:
        p = lax.rem(t0 - base, seq_len)
        if pos is None:
            pos, length = p, jnp.int32(seq_len)
        else:
            pos = jnp.where(t0 >= base, p, pos)
            length = jnp.where(t0 >= base, jnp.int32(seq_len), length)
        base += n_seq * seq_len
    return pos, length


def _const_spec(shape):
    nd = len(shape)
    return pl.BlockSpec(shape, lambda *_: (0,) * nd, pipeline_mode=pl.Buffered(1))


def _params(sem):
    return pltpu.CompilerParams(dimension_semantics=sem, vmem_limit_bytes=VMEM_LIMIT)


def _proj_kernel(segs, tm, x_ref, xp_ref, xn_ref, g_ref, wg_ref, gb_ref, wz_ref, wx_ref, cw_ref, cb_ref,
                 wdt_ref, wr_ref, rope_ref, wn_ref, gates_ref, ssd_ref, dt_ref, ret_ref, na_ref):
    i = pl.program_id(0)
    pos0, seq_len = _tile_pos(i, tm, segs)
    g = g_ref[...]

    def rms(v):
        return v * lax.rsqrt(jnp.mean(v * v, axis=-1, keepdims=True) + EPS) * g

    h = rms(x_ref[...]).astype(BF16)
    prev_ok = (pos0 > 0).astype(F32)
    next_ok = (pos0 + tm < seq_len).astype(F32)
    hp = (rms(xp_ref[...]) * prev_ok).astype(BF16)
    hn = (rms(xn_ref[...]) * next_ok).astype(BF16)
    hcat = jnp.concatenate([hp, h, hn], axis=0)

    for c in range(GATE_W // 512):
        sl = slice(c * 512, (c + 1) * 512)
        a = _dot(h, wg_ref[:, sl]) + gb_ref[:, sl]
        gates_ref[:, sl] = _sigmoid(a).astype(BF16)

    ssd_ref[:, 0:SSD_WIDTH] = _dot(h, wz_ref[...]).astype(BF16)
    for c in range(SSD_XBC // 512):
        sl = slice(c * 512, (c + 1) * 512)
        u = _dot(hcat, wx_ref[:, sl])
        acc = cb_ref[:, sl] + cw_ref[0:1, sl] * u[HALO - 2:HALO - 2 + tm]
        for k in range(1, SSD_CONV):
            acc = acc + cw_ref[k:k + 1, sl] * u[HALO - 2 + k:HALO - 2 + k + tm]
        ssd_ref[:, SSD_WIDTH + c * 512:SSD_WIDTH + (c + 1) * 512] = _silu(acc).astype(BF16)

    dt_t = _dot_nt(wdt_ref[...], h)
    for k in range(tm // CHUNK):
        dt_ref[k] = dt_t[:, k * CHUNK:(k + 1) * CHUNK]

    cos = rope_ref[:, 0:128]
    sin = rope_ref[:, 128:256]
    lane = lax.broadcasted_iota(jnp.int32, (tm, 128), 1)
    first_half = (lane & (RET_QK - 1)) < RET_QK // 2

    def rotary(t, scale):
        outs = []
        for j in range(RET_QK_W // 128):
            tj = t[:, j * 128:(j + 1) * 128]
            partner = jnp.where(first_half, pltpu.roll(tj, 128 - RET_QK // 2, axis=1),
                                pltpu.roll(tj, RET_QK // 2, axis=1))
            outs.append((tj * cos + partner * sin) * scale)
        return jnp.concatenate(outs, axis=1)

    ret_ref[:, 0:RET_QK_W] = rotary(_dot(h, wr_ref[:, 0:RET_QK_W]), 1.0).astype(BF16)
    ret_ref[:, RET_QK_W:2 * RET_QK_W] = rotary(_dot(h, wr_ref[:, RET_QK_W:2 * RET_QK_W]),
                                              RET_QK ** -0.5).astype(BF16)
    for c in range(2):
        sl = slice(2 * RET_QK_W + c * 512, 2 * RET_QK_W + (c + 1) * 512)
        ret_ref[:, sl] = _dot(h, wr_ref[:, sl]).astype(BF16)

    for c in range(NA_IN_W // 512):
        sl = slice(c * 512, (c + 1) * 512)
        na_ref[:, sl] = _dot(h, wn_ref[:, sl]).astype(BF16)


def _halo_specs(tm, n_tok):
    r = tm // HALO
    last = n_tok // HALO - 1
    prev = pl.BlockSpec((HALO, D_MODEL), lambda i: (jnp.maximum(i * r - 1, 0), 0))
    nxt = pl.BlockSpec((HALO, D_MODEL), lambda i: (jnp.minimum((i + 1) * r, last), 0))
    return prev, nxt


def _proj_call(x, lw, rope, segs, tm):
    n_tok = x.shape[0]

    def rope_map(i):
        pos0, _ = _tile_pos(i, tm, segs)
        return (pos0 // tm, 0)

    prev_spec, next_spec = _halo_specs(tm, n_tok)
    tok = lambda w: pl.BlockSpec((tm, w), lambda i: (i, 0))
    return pl.pallas_call(
        functools.partial(_proj_kernel, segs, tm),
        grid=(n_tok // tm,),
        in_specs=[tok(D_MODEL), prev_spec, next_spec,
                  _const_spec((1, D_MODEL)),
                  _const_spec((D_MODEL, GATE_W)), _const_spec((1, GATE_W)),
                  _const_spec((D_MODEL, SSD_WIDTH)),
                  _const_spec((D_MODEL, SSD_XBC)), _const_spec((SSD_CONV, SSD_XBC)), _const_spec((1, SSD_XBC)),
                  _const_spec((2 * SSD_HEADS, D_MODEL)),
                  _const_spec((D_MODEL, RET_IN_W)),
                  pl.BlockSpec((tm, 256), rope_map),
                  _const_spec((D_MODEL, NA_IN_W))],
        out_specs=[tok(GATE_W), tok(SSD_IN_W),
                   pl.BlockSpec((tm // CHUNK, 2 * SSD_HEADS, CHUNK), lambda i: (i, 0, 0)),
                   tok(RET_IN_W), tok(NA_IN_W)],
        out_shape=[jax.ShapeDtypeStruct((n_tok, GATE_W), BF16),
                   jax.ShapeDtypeStruct((n_tok, SSD_IN_W), BF16),
                   jax.ShapeDtypeStruct((n_tok // CHUNK, 2 * SSD_HEADS, CHUNK), F32),
                   jax.ShapeDtypeStruct((n_tok, RET_IN_W), BF16),
                   jax.ShapeDtypeStruct((n_tok, NA_IN_W), BF16)],
        compiler_params=_params(("arbitrary",)),
        name="proj",
    )(x, x, x, lw["norm_mix"], lw["w_gate"], lw["gate_bias"], lw["w_z"], lw["w_xbc"], lw["conv_w"],
      lw["conv_b"], lw["w_dt"], lw["w_ret"], rope, lw["w_na"])


def _ssd_kernel(n_chunks, in_ref, dt_ref, dtb_ref, a_ref, dsk_ref, ng_ref, out_ref,
                sb_ref, sf_state, sb_state, exp_ref):
    q = CHUNK
    nh = SSD_HEADS
    gw = SSD_WIDTH // SSD_GROUPS
    ii = lax.broadcasted_iota(jnp.int32, (q, q), 0)
    jj = lax.broadcasted_iota(jnp.int32, (q, q), 1)
    causal = jj <= ii
    pre_m = (ii <= jj).astype(BF16)
    suf_m = (ii >= jj).astype(BF16)
    row16 = lax.broadcasted_iota(jnp.int32, (2 * nh, q), 0)
    is_fwd_row = row16 < nh
    lane_g = lax.broadcasted_iota(jnp.int32, (q, gw), 1)

    er = lax.broadcasted_iota(jnp.int32, (q, 4 * SSD_WIDTH), 0)
    ec = lax.broadcasted_iota(jnp.int32, (q, 4 * SSD_WIDTH), 1)
    ecc = ec & (2 * SSD_WIDTH - 1)
    hd = ((ecc >> 9) << 3) + ((ecc & (SSD_WIDTH - 1)) >> 6)
    exp_ref[...] = (er == jnp.where(ec < 2 * SSD_WIDTH, 2 * nh, 4 * nh) + hd).astype(BF16)

    dtb = dtb_ref[...]
    a_neg = a_ref[...]

    def prep(c):
        v = dt_ref[c] + dtb
        dtv = jnp.maximum(v, 0.0) + jnp.log1p(jnp.exp(-jnp.abs(v)))
        la = dtv * a_neg
        pre = _dot3(la, pre_m)
        suf = _dot3(la, suf_m)
        cum = jnp.where(is_fwd_row, pre, suf)
        tot = jnp.where(is_fwd_row[:, 0:1], pre[:, q - 1:q], suf[:, 0:1])
        wgt = jnp.exp(tot - cum) * dtv
        esc = jnp.exp(cum)
        pack = jnp.concatenate([cum, wgt, esc, jnp.zeros((q - 6 * nh, q), F32)], axis=0)
        return dtv, cum, pack.T

    def group_b_t(blk, gidx):
        b = blk[:, SSD_WIDTH + SSD_WIDTH + gidx * SSD_STATE:SSD_WIDTH + SSD_WIDTH + (gidx + 1) * SSD_STATE]
        return b.astype(F32).T.astype(BF16)

    def state_update(blk, xs, wexp, state_ref, decay_row):
        xw = (xs * wexp).astype(BF16)
        ds = [_dot(group_b_t(blk, gidx), xw[:, gidx * gw:(gidx + 1) * gw]) for gidx in range(SSD_GROUPS)]
        state_ref[...] = state_ref[...] * decay_row + jnp.concatenate(ds, axis=1)

    sb_state[...] = jnp.zeros_like(sb_state)

    def back_body(t, carry):
        c = n_chunks - 1 - t
        r0 = pl.multiple_of(c * q, q)
        blk = in_ref[pl.ds(r0, q), :]
        sb_ref[c] = sb_state[...].astype(BF16)
        _, _, pt = prep(c)
        ptb = pt.astype(BF16)
        w_b = _dot(ptb, exp_ref[:, SSD_WIDTH:2 * SSD_WIDTH])
        e_b = _dot(ptb, exp_ref[:, 3 * SSD_WIDTH:4 * SSD_WIDTH])
        xs = blk[:, SSD_WIDTH:2 * SSD_WIDTH].astype(F32)
        state_update(blk, xs, w_b, sb_state, e_b[0:1, :])
        return carry

    lax.fori_loop(0, n_chunks, back_body, 0)

    sf_state[...] = jnp.zeros_like(sf_state)

    def fwd_body(c, carry):
        r0 = pl.multiple_of(c * q, q)
        blk = in_ref[pl.ds(r0, q), :]
        z = blk[:, 0:SSD_WIDTH].astype(F32)
        xs_b = blk[:, SSD_WIDTH:2 * SSD_WIDTH]
        xs = xs_b.astype(F32)
        dtv, cum, pt = prep(c)
        we = _dot(pt.astype(BF16), exp_ref[...])
        w_f = we[:, 0:SSD_WIDTH]
        e_f = we[:, 2 * SSD_WIDTH:3 * SSD_WIDTH]
        e_b = we[:, 3 * SSD_WIDTH:4 * SSD_WIDTH]

        ys = []
        for gidx in range(SSD_GROUPS):
            c_off = SSD_WIDTH + SSD_WIDTH + SSD_BC + gidx * SSD_STATE
            b_off = SSD_WIDTH + SSD_WIDTH + gidx * SSD_STATE
            cg = blk[:, c_off:c_off + SSD_STATE]
            bg = blk[:, b_off:b_off + SSD_STATE]
            scores = _dot_nt(cg, bg)
            xg = xs_b[:, gidx * gw:(gidx + 1) * gw]
            yg = jnp.zeros((q, gw), F32)
            for r in range(SSD_HEADS // SSD_GROUPS):
                hf = gidx * (SSD_HEADS // SSD_GROUPS) + r
                hb = nh + hf
                seg_f = jnp.broadcast_to(pt[:, hf:hf + 1], (q, q)) - jnp.broadcast_to(cum[hf:hf + 1, :], (q, q))
                seg_b = jnp.broadcast_to(pt[:, hb:hb + 1], (q, q)) - jnp.broadcast_to(cum[hb:hb + 1, :], (q, q))
                seg = jnp.where(causal, seg_f, seg_b)
                dcol = jnp.where(causal, jnp.broadcast_to(dtv[hf:hf + 1, :], (q, q)),
                                 jnp.broadcast_to(dtv[hb:hb + 1, :], (q, q)))
                m = (scores * jnp.exp(seg) * dcol).astype(BF16)
                yr = _dot(m, xg)
                yg = jnp.where((lane_g >> 6) == r, yr, yg)
            sl = slice(gidx * gw, (gidx + 1) * gw)
            y_inter = (_dot(cg, sf_state[:, sl].astype(BF16)) * e_f[:, sl]
                       + _dot(cg, sb_ref[c][:, sl]) * e_b[:, sl])
            ys.append(yg + y_inter)
        y = jnp.concatenate(ys, axis=1) + dsk_ref[...] * xs
        o = y * _silu(z)
        o = o * lax.rsqrt(jnp.mean(o * o, axis=-1, keepdims=True) + EPS) * ng_ref[...]
        out_ref[pl.ds(r0, q), :] = o.astype(BF16)
        state_update(blk, xs, w_f, sf_state, e_f[q - 1:q, :])
        return carry

    lax.fori_loop(0, n_chunks, fwd_body, 0)


def _seq_call(kernel_fn, name, inputs, in_specs, out_width, n_tok, n_seq, seq_len, blk0, scratch, prev_out):
    out_spec = pl.BlockSpec((seq_len, out_width), lambda s: (blk0 + s, 0))
    kwargs = {}
    if prev_out is not None:
        inputs = list(inputs) + [prev_out]
        in_specs = list(in_specs) + [pl.BlockSpec(memory_space=pl.ANY)]
        kwargs["input_output_aliases"] = {len(inputs) - 1: 0}
        body = kernel_fn
        kernel_fn = lambda *refs: body(*refs[:len(inputs) - 1], *refs[len(inputs):])
    return pl.pallas_call(
        kernel_fn,
        grid=(n_seq,),
        in_specs=in_specs,
        out_specs=out_spec,
        out_shape=jax.ShapeDtypeStruct((n_tok, out_width), BF16),
        scratch_shapes=scratch,
        compiler_params=_params(("arbitrary",)),
        name=name,
        **kwargs,
    )(*inputs)


def _over_segments(segs, make_call):
    out = None
    base = 0
    for idx, (n_seq, seq_len) in enumerate(segs):
        assert base % seq_len == 0
        out = make_call(idx, n_seq, seq_len, base // seq_len, out)
        base += n_seq * seq_len
    return out


def _ssd_call(ssd_in, dt_raw, lw, segs):
    n_tok = ssd_in.shape[0]

    def make(idx, n_seq, seq_len, blk0, prev_out):
        n_chunks = seq_len // CHUNK
        in_specs = [pl.BlockSpec((seq_len, SSD_IN_W), lambda s: (blk0 + s, 0)),
                    pl.BlockSpec((n_chunks, 2 * SSD_HEADS, CHUNK), lambda s: (blk0 + s, 0, 0)),
                    _const_spec((2 * SSD_HEADS, 1)), _const_spec((2 * SSD_HEADS, 1)),
                    _const_spec((1, SSD_WIDTH)), _const_spec((1, SSD_WIDTH))]
        scratch = [pltpu.VMEM((n_chunks, SSD_STATE, SSD_WIDTH), BF16),
                   pltpu.VMEM((SSD_STATE, SSD_WIDTH), F32),
                   pltpu.VMEM((SSD_STATE, SSD_WIDTH), F32),
                   pltpu.VMEM((CHUNK, 4 * SSD_WIDTH), BF16)]
        return _seq_call(functools.partial(_ssd_kernel, n_chunks), f"ssd{idx}",
                         [ssd_in, dt_raw, lw["dt_bias"], lw["a_neg"], lw["d_skip"], lw["ssd_norm"]],
                         in_specs, SSD_WIDTH, n_tok, n_seq, seq_len, blk0, scratch, prev_out)

    return _over_segments(segs, make)


def _ret_kernel(n_chunks, in_ref, thq_ref, thv_ref, out_ref, sb_ref, sf_state, sb_state):
    q = CHUNK
    nh = RET_HEADS
    lgq = -jnp.exp(thq_ref[...])
    lgv = -jnp.exp(thv_ref[...])
    ii = lax.broadcasted_iota(jnp.int32, (q, q), 0)
    jj = lax.broadcasted_iota(jnp.int32, (q, q), 1)
    causal = jj <= ii
    dist = (ii - jj).astype(F32)
    tok = lax.broadcasted_iota(jnp.int32, (q, 1), 0).astype(F32)

    w_f = jnp.exp((q - 1.0 - tok) * lgq[0:1, :])
    w_b = jnp.exp(tok * lgq[1:2, :])
    e_f = jnp.exp((tok + 1.0) * lgv[0:1, :])
    e_b = jnp.exp((q - tok) * lgv[1:2, :])
    decays = []
    dec_f, dec_b = [], []
    for h in range(nh):
        lf = lgv[0:1, h * RET_V:(h + 1) * RET_V]
        lb = lgv[1:2, h * RET_V:(h + 1) * RET_V]
        decays.append(jnp.where(causal, jnp.exp(dist * lf), jnp.exp(-dist * lb)))
        dec_f.append(jnp.broadcast_to(jnp.exp(float(q) * lf), (RET_QK, RET_V)))
        dec_b.append(jnp.broadcast_to(jnp.exp(float(q) * lb), (RET_QK, RET_V)))
    dec_f = jnp.concatenate(dec_f, axis=0)
    dec_b = jnp.concatenate(dec_b, axis=0)
    lane_q = lax.broadcasted_iota(jnp.int32, (q, RET_QK_W), 1) >> 6

    def state_update(k, v, wrow, state_ref, dec):
        kw = (k.astype(F32) * wrow).T.astype(BF16)
        full = _dot(kw, v)
        ds = jnp.concatenate([full[h * RET_QK:(h + 1) * RET_QK, h * RET_V:(h + 1) * RET_V] for h in range(nh)],
                             axis=0)
        state_ref[...] = state_ref[...] * dec + ds

    sb_state[...] = jnp.zeros_like(sb_state)

    def back_body(t, carry):
        c = n_chunks - 1 - t
        r0 = pl.multiple_of(c * q, q)
        blk = in_ref[pl.ds(r0, q), :]
        sb_ref[c] = sb_state[...].astype(BF16)
        state_update(blk[:, RET_QK_W:2 * RET_QK_W], blk[:, 2 * RET_QK_W:2 * RET_QK_W + RET_WIDTH],
                     w_b, sb_state, dec_b)
        return carry

    lax.fori_loop(0, n_chunks, back_body, 0)

    sf_state[...] = jnp.zeros_like(sf_state)

    def fwd_body(c, carry):
        r0 = pl.multiple_of(c * q, q)
        blk = in_ref[pl.ds(r0, q), :]
        qv = blk[:, 0:RET_QK_W]
        kv = blk[:, RET_QK_W:2 * RET_QK_W]
        vv = blk[:, 2 * RET_QK_W:2 * RET_QK_W + RET_WIDTH]
        gv = blk[:, 2 * RET_QK_W + RET_WIDTH:RET_IN_W].astype(F32)
        zero = jnp.zeros_like(qv)
        k_bd = jnp.concatenate([jnp.where(lane_q == h, kv, zero) for h in range(nh)], axis=0)
        s_all = _dot_nt(qv, k_bd)
        sf_b = sf_state[...].astype(BF16)
        sb_b = sb_ref[c]
        outs = []
        for h in range(nh):
            sl = slice(h * RET_V, (h + 1) * RET_V)
            p = (s_all[:, h * q:(h + 1) * q] * decays[h]).astype(BF16)
            qh = jnp.where(lane_q == h, qv, zero)
            y = (_dot(p, vv[:, sl]) + _dot(qh, sf_b) * e_f[:, sl] + _dot(qh, sb_b) * e_b[:, sl])
            outs.append(y * lax.rsqrt(jnp.mean(y * y, axis=-1, keepdims=True) + EPS))
        o = jnp.concatenate(outs, axis=1) * _silu(gv)
        out_ref[pl.ds(r0, q), :] = o.astype(BF16)
        state_update(kv, vv, w_f, sf_state, dec_f)
        return carry

    lax.fori_loop(0, n_chunks, fwd_body, 0)


def _ret_call(ret_in, lw, segs):
    n_tok = ret_in.shape[0]

    def make(idx, n_seq, seq_len, blk0, prev_out):
        n_chunks = seq_len // CHUNK
        in_specs = [pl.BlockSpec((seq_len, RET_IN_W), lambda s: (blk0 + s, 0)),
                    _const_spec((2, RET_QK_W)), _const_spec((2, RET_WIDTH))]
        scratch = [pltpu.VMEM((n_chunks, RET_QK_W, RET_V), BF16),
                   pltpu.VMEM((RET_QK_W, RET_V), F32),
                   pltpu.VMEM((RET_QK_W, RET_V), F32)]
        return _seq_call(functools.partial(_ret_kernel, n_chunks), f"ret{idx}",
                         [ret_in, lw["theta_q"], lw["theta_v"]],
                         in_specs, RET_WIDTH, n_tok, n_seq, seq_len, blk0, scratch, prev_out)

    return _over_segments(segs, make)


def _na_kernel(n_rows, in_ref, bias_ref, out_ref):
    w = GRID_W
    lane = lax.broadcasted_iota(jnp.int32, (w, 128), 1)
    low = lane < NA_HEADDIM
    scale = NA_HEADDIM ** -0.5

    def row_body(r, carry):
        rs = jnp.clip(r - NA_ROWS // 2, 0, n_rows - NA_ROWS)
        d = r - rs
        q0 = pl.multiple_of(r * w, w)
        k0 = pl.multiple_of(rs * w, w)
        for p in range(NA_HEADS // 2):
            qp = in_ref[pl.ds(q0, w), p * 128:(p + 1) * 128]
            zero = jnp.zeros_like(qp)
            lhs = jnp.concatenate([jnp.where(low, qp, zero), jnp.where(low, zero, qp)], axis=0)
            kp = in_ref[pl.ds(k0, NA_WIN), NA_WIDTH + p * 128:NA_WIDTH + (p + 1) * 128]
            s = _dot_nt(lhs, kp) * scale + bias_ref[d, p]
            m = jnp.max(s, axis=-1, keepdims=True)
            e = jnp.exp(s - m)
            l = jnp.sum(e, axis=-1, keepdims=True)
            vp = in_ref[pl.ds(k0, NA_WIN), 2 * NA_WIDTH + p * 128:2 * NA_WIDTH + (p + 1) * 128]
            o = _dot(e.astype(BF16), vp) / l
            out_ref[pl.ds(q0, w), p * 128:(p + 1) * 128] = jnp.where(low, o[0:w], o[w:2 * w]).astype(BF16)
        return carry

    lax.fori_loop(0, n_rows, row_body, 0)


def _na_call(na_in, bias, segs):
    n_tok = na_in.shape[0]

    def make(idx, n_seq, seq_len, blk0, prev_out):
        n_rows = seq_len // GRID_W
        assert n_rows >= NA_ROWS
        in_specs = [pl.BlockSpec((seq_len, NA_IN_W), lambda s: (blk0 + s, 0)),
                    _const_spec((NA_ROWS, NA_HEADS // 2, 2 * GRID_W, NA_WIN))]
        return _seq_call(functools.partial(_na_kernel, n_rows), f"na{idx}", [na_in, bias],
                         in_specs, NA_WIDTH, n_tok, n_seq, seq_len, blk0, [], prev_out)

    return _over_segments(segs, make)


def _na_bias_table(rpb):
    cols = np.arange(GRID_W)
    col_start = np.clip(cols - NA_COLS // 2, 0, GRID_W - NA_COLS)
    kc = np.arange(GRID_W)
    valid = (kc[None, :] >= col_start[:, None]) & (kc[None, :] < col_start[:, None] + NA_COLS)
    col_off = np.clip(kc[None, :] - cols[:, None] + NA_COLS - 1, 0, 2 * NA_COLS - 2)
    toep = jnp.where(valid[None, None], rpb.astype(F32)[:, :, col_off], NEG_BIG)
    per_shift = []
    for d in range(NA_ROWS):
        rows = [toep[:, x - d + NA_ROWS - 1] for x in range(NA_ROWS)]
        per_shift.append(jnp.stack(rows, axis=2).reshape(NA_HEADS, GRID_W, NA_WIN))
    table = jnp.stack(per_shift, axis=0)
    return table.reshape(NA_ROWS, NA_HEADS // 2, 2 * GRID_W, NA_WIN)


def _merge_kernel(x_ref, gates_ref, o0_ref, o1_ref, o2_ref, wb_ref, wo_ref, out_ref):
    merged = None
    for b, o_ref in enumerate((o0_ref, o1_ref, o2_ref)):
        t = gates_ref[:, b * D_MODEL:(b + 1) * D_MODEL].astype(F32) * _dot(o_ref[...], wb_ref[b])
        merged = t if merged is None else merged + t
    out_ref[...] = x_ref[...] + _dot(merged.astype(BF16), wo_ref[...])


def _merge_call(x, gates, o_ssd, o_ret, o_na, lw, tm):
    n_tok = x.shape[0]
    tok = lambda w: pl.BlockSpec((tm, w), lambda i: (i, 0))
    return pl.pallas_call(
        _merge_kernel,
        grid=(n_tok // tm,),
        in_specs=[tok(D_MODEL), tok(GATE_W), tok(SSD_WIDTH), tok(RET_WIDTH), tok(NA_WIDTH),
                  _const_spec((N_BRANCH, SSD_WIDTH, D_MODEL)), _const_spec((D_MODEL, D_MODEL))],
        out_specs=tok(D_MODEL),
        out_shape=jax.ShapeDtypeStruct((n_tok, D_MODEL), F32),
        compiler_params=_params(("arbitrary",)),
        name="merge",
    )(x, gates, o_ssd, o_ret, o_na, lw["w_branch"], lw["w_out"])


def _ffn_kernel(segs, tm, final, x_ref, xp_ref, xn_ref, g_ref, wup_ref, cw_ref, cb_ref, wdn_ref, gf_ref,
                out_ref, h_ref, acc_ref):
    i = pl.program_id(0)
    pos0, seq_len = _tile_pos(i, tm, segs)
    g = g_ref[...]

    def rms(v):
        return v * lax.rsqrt(jnp.mean(v * v, axis=-1, keepdims=True) + EPS) * g

    x = x_ref[...]
    prev_ok = (pos0 > 0).astype(F32)
    next_ok = (pos0 + tm < seq_len).astype(F32)
    h_ref[0:HALO, :] = (rms(xp_ref[...]) * prev_ok).astype(BF16)
    h_ref[HALO:HALO + tm, :] = rms(x).astype(BF16)
    h_ref[HALO + tm:2 * HALO + tm, :] = (rms(xn_ref[...]) * next_ok).astype(BF16)
    acc_ref[...] = jnp.zeros_like(acc_ref)

    def conv(u, cw, cb):
        acc = cb + cw[0:1, :] * u[HALO - 1:HALO - 1 + tm]
        for k in range(1, FFN_CONV):
            acc = acc + cw[k:k + 1, :] * u[HALO - 1 + k:HALO - 1 + k + tm]
        return acc

    def body(c, carry):
        hcat = h_ref[...]
        val = conv(_dot(hcat, wup_ref[c]), cw_ref[c], cb_ref[c])
        gate = conv(_dot(hcat, wup_ref[N_FF_CHUNKS + c]), cw_ref[N_FF_CHUNKS + c], cb_ref[N_FF_CHUNKS + c])
        act = (_silu(gate) * val).astype(BF16)
        acc_ref[...] += _dot(act, wdn_ref[c])
        return carry

    lax.fori_loop(0, N_FF_CHUNKS, body, 0)
    y = x + acc_ref[...]
    if final:
        y = y * lax.rsqrt(jnp.mean(y * y, axis=-1, keepdims=True) + EPS) * gf_ref[...]
    out_ref[...] = y


def _ffn_call(x, lw, norm_final, segs, tm, final):
    n_tok = x.shape[0]
    prev_spec, next_spec = _halo_specs(tm, n_tok)
    tok = pl.BlockSpec((tm, D_MODEL), lambda i: (i, 0))
    return pl.pallas_call(
        functools.partial(_ffn_kernel, segs, tm, final),
        grid=(n_tok // tm,),
        in_specs=[tok, prev_spec, next_spec, _const_spec((1, D_MODEL)),
                  _const_spec((2 * N_FF_CHUNKS, D_MODEL, FF_CHUNK)),
                  _const_spec((2 * N_FF_CHUNKS, FFN_CONV, FF_CHUNK)),
                  _const_spec((2 * N_FF_CHUNKS, 1, FF_CHUNK)),
                  _const_spec((N_FF_CHUNKS, FF_CHUNK, D_MODEL)),
                  _const_spec((1, D_MODEL))],
        out_specs=tok,
        out_shape=jax.ShapeDtypeStruct((n_tok, D_MODEL), F32),
        scratch_shapes=[pltpu.VMEM((tm + 2 * HALO, D_MODEL), BF16), pltpu.VMEM((tm, D_MODEL), F32)],
        compiler_params=_params(("arbitrary",)),
        name="ffn",
    )(x, x, x, lw["norm_ffn"], lw["w_up"], lw["ffn_conv_w"], lw["ffn_conv_b"], lw["w_down"], norm_final)


def _layer_weights(i, norm_mix, w_in, gate_bias, ssd_conv_w, ssd_conv_b, ssd_dt_bias, ssd_a_log, ssd_d, ssd_norm,
                   ret_theta, w_branch, w_out, norm_ffn, ffn_w_up, ffn_conv_w, ffn_conv_b, ffn_w_down):
    w = w_in[i]
    offs = np.cumsum([0, SSD_WIDTH, SSD_XBC, 2 * SSD_HEADS, RET_IN_W, NA_IN_W, GATE_W])
    col = lambda k: w[:, offs[k]:offs[k + 1]]
    up = ffn_w_up[i].astype(BF16).reshape(D_MODEL, 2 * N_FF_CHUNKS, FF_CHUNK).transpose(1, 0, 2)
    return {
        "norm_mix": norm_mix[i].reshape(1, D_MODEL),
        "w_z": col(0).astype(BF16),
        "w_xbc": col(1).astype(BF16),
        "w_dt": col(2).T.astype(BF16),
        "w_ret": col(3).astype(BF16),
        "w_na": col(4).astype(BF16),
        "w_gate": col(5).astype(BF16),
        "gate_bias": gate_bias[i].reshape(1, GATE_W),
        "conv_w": ssd_conv_w[i],
        "conv_b": ssd_conv_b[i].reshape(1, SSD_XBC),
        "dt_bias": ssd_dt_bias[i].reshape(2 * SSD_HEADS, 1),
        "a_neg": -jnp.exp(ssd_a_log[i].astype(F32)).reshape(2 * SSD_HEADS, 1),
        "d_skip": jnp.repeat(ssd_d[i], SSD_HEADDIM).reshape(1, SSD_WIDTH),
        "ssd_norm": ssd_norm[i].reshape(1, SSD_WIDTH),
        "theta_q": jnp.repeat(ret_theta[i], RET_QK, axis=1),
        "theta_v": jnp.repeat(ret_theta[i], RET_V, axis=1),
        "w_branch": w_branch[i].astype(BF16),
        "w_out": w_out[i].astype(BF16),
        "norm_ffn": norm_ffn[i].reshape(1, D_MODEL),
        "w_up": up,
        "ffn_conv_w": ffn_conv_w[i].reshape(FFN_CONV, 2 * N_FF_CHUNKS, FF_CHUNK).transpose(1, 0, 2),
        "ffn_conv_b": ffn_conv_b[i].reshape(2 * N_FF_CHUNKS, 1, FF_CHUNK),
        "w_down": ffn_w_down[i].astype(BF16).reshape(N_FF_CHUNKS, FF_CHUNK, D_MODEL),
    }


def _rope_table(max_len):
    half = RET_QK // 2
    inv = 1.0 / (ROPE_BASE ** (jnp.arange(half, dtype=F32) / half))
    ang = jnp.arange(max_len, dtype=F32)[:, None] * inv[None, :]
    cos = jnp.tile(jnp.cos(ang), (1, 128 // half))
    sin = jnp.tile(jnp.concatenate([-jnp.sin(ang), jnp.sin(ang)], axis=1), (1, 128 // RET_QK))
    return jnp.concatenate([cos, sin], axis=1)


def _pick_tile(segs):
    tm = 512
    while any(seq_len % tm for _, seq_len in segs):
        tm //= 2
    assert tm >= CHUNK
    return tm


def _trunk(xs, norm_mix, w_in, gate_bias, ssd_conv_w, ssd_conv_b, ssd_dt_bias, ssd_a_log, ssd_d, ssd_norm,
           ret_theta, na_rpb, w_branch, w_out, norm_ffn, ffn_w_up, ffn_conv_w, ffn_conv_b, ffn_w_down, norm_final):
    segs = tuple((int(x.shape[0]), int(x.shape[1])) for x in xs)
    tm = _pick_tile(segs)
    x = jnp.concatenate([x.reshape(-1, D_MODEL) for x in xs], axis=0)
    rope = _rope_table(max(seq_len for _, seq_len in segs))
    depth = w_in.shape[0]
    gf = norm_final.reshape(1, D_MODEL)
    for i in range(depth):
        lw = _layer_weights(i, norm_mix, w_in, gate_bias, ssd_conv_w, ssd_conv_b, ssd_dt_bias, ssd_a_log, ssd_d,
                            ssd_norm, ret_theta, w_branch, w_out, norm_ffn, ffn_w_up, ffn_conv_w, ffn_conv_b,
                            ffn_w_down)
        gates, ssd_in, dt_raw, ret_in, na_in = _proj_call(x, lw, rope, segs, tm)
        o_ssd = _ssd_call(ssd_in, dt_raw, lw, segs)
        o_ret = _ret_call(ret_in, lw, segs)
        o_na = _na_call(na_in, _na_bias_table(na_rpb[i]), segs)
        x = _merge_call(x, gates, o_ssd, o_ret, o_na, lw, tm)
        x = _ffn_call(x, lw, gf, segs, tm, final=(i == depth - 1))
    outs = []
    base = 0
    for (n_seq, seq_len), x_in in zip(segs, xs):
        outs.append(x[base:base + n_seq * seq_len].reshape(x_in.shape))
        base += n_seq * seq_len
    return tuple(outs)


def kernel(x_prompt, x_sample, norm_mix, w_in, gate_bias, ssd_conv_w, ssd_conv_b, ssd_dt_bias, ssd_a_log, ssd_d,
           ssd_norm, ret_theta, na_rpb, w_branch, w_out, norm_ffn, ffn_w_up, ffn_conv_w, ffn_conv_b, ffn_w_down,
           norm_final):
    return _trunk((x_prompt, x_sample), norm_mix, w_in, gate_bias, ssd_conv_w, ssd_conv_b, ssd_dt_bias, ssd_a_log,
                  ssd_d, ssd_norm, ret_theta, na_rpb, w_branch, w_out, norm_ffn, ffn_w_up, ffn_conv_w, ffn_conv_b,
                  ffn_w_down, norm_final)
```

```python
import functools

import jax
import jax.numpy as jnp
import numpy as np
from jax import lax
from jax.experimental import pallas as pl
from jax.experimental.pallas import tpu as pltpu

F32 = jnp.float32
BF16 = jnp.bfloat16

D_MODEL = 1024
GRID_W = 64
CHUNK = 128
SSD_HEADS = 8
SSD_HEADDIM = 64
SSD_WIDTH = SSD_HEADS * SSD_HEADDIM
SSD_GROUPS = 2
SSD_STATE = 128
SSD_BC = SSD_GROUPS * SSD_STATE
SSD_XBC = SSD_WIDTH + 2 * SSD_BC
SSD_CONV = 5
RET_HEADS = 4
RET_QK = 64
RET_V = 128
RET_WIDTH = RET_HEADS * RET_V
ROPE_BASE = 10000.0
NA_HEADS = 8
NA_HEADDIM = 64
NA_WIDTH = NA_HEADS * NA_HEADDIM
NA_ROWS = 8
NA_COLS = 16
N_BRANCH = 3
D_FF = 2816
FFN_CONV = 3
EPS = 1e-6

GATE_W = N_BRANCH * D_MODEL
SSD_IN_W = SSD_WIDTH + SSD_XBC
RET_QK_W = RET_HEADS * RET_QK
RET_IN_W = 2 * RET_QK_W + 2 * RET_WIDTH
NA_IN_W = 3 * NA_WIDTH
NA_WIN = NA_ROWS * GRID_W
HALO = 16
FF_CHUNK = 256
N_FF_CHUNKS = D_FF // FF_CHUNK
NEG_BIG = -1e30
VMEM_LIMIT = 56 * 1024 * 1024


def _sigmoid(v):
    return 1.0 / (1.0 + jnp.exp(-v))


def _silu(v):
    return v * _sigmoid(v)


def _dot(a, b):
    return jnp.dot(a, b, preferred_element_type=F32)


def _dot_nt(a, b):
    return lax.dot_general(a, b, (((1,), (1,)), ((), ())), preferred_element_type=F32)


def _dot3(x, m01):
    x1 = x.astype(BF16)
    r1 = x - x1.astype(F32)
    x2 = r1.astype(BF16)
    x3 = (r1 - x2.astype(F32)).astype(BF16)
    return _dot(x1, m01) + _dot(x2, m01) + _dot(x3, m01)


def _tile_pos(i, tm, segs):
    t0 = i * tm
    pos = None
    length = None
    base = 0
    for n_seq, seq_len in segs:
        p = lax.rem(t0 - base, seq_len)
        if pos is None:
            pos, length = p, jnp.int32(seq_len)
        else:
            pos = jnp.where(t0 >= base, p, pos)
            length = jnp.where(t0 >= base, jnp.int32(seq_len), length)
        base += n_seq * seq_len
    return pos, length


def _const_spec(shape):
    nd = len(shape)
    return pl.BlockSpec(shape, lambda *_: (0,) * nd, pipeline_mode=pl.Buffered(1))


def _params(sem):
    return pltpu.CompilerParams(dimension_semantics=sem, vmem_limit_bytes=VMEM_LIMIT)


def _proj_kernel(segs, tm, x_ref, xp_ref, xn_ref, g_ref, wg_ref, gb_ref, wz_ref, wx_ref, cw_ref, cb_ref,
                 wdt_ref, wr_ref, rope_ref, wn_ref, gates_ref, ssd_ref, dt_ref, ret_ref, na_ref, u_ref):
    i = pl.program_id(0)
    pos0, seq_len = _tile_pos(i, tm, segs)
    g = g_ref[...]

    def rms(v):
        return v * lax.rsqrt(jnp.mean(v * v, axis=-1, keepdims=True) + EPS) * g

    h = rms(x_ref[...]).astype(BF16)
    prev_ok = (pos0 > 0).astype(F32)
    next_ok = (pos0 + tm < seq_len).astype(F32)
    hp = (rms(xp_ref[...]) * prev_ok).astype(BF16)
    hn = (rms(xn_ref[...]) * next_ok).astype(BF16)
    hcat = jnp.concatenate([hp, h, hn], axis=0)

    for c in range(GATE_W // 512):
        sl = slice(c * 512, (c + 1) * 512)
        a = _dot(h, wg_ref[:, sl]) + gb_ref[:, sl]
        gates_ref[:, sl] = _sigmoid(a).astype(BF16)

    ssd_ref[:, 0:SSD_WIDTH] = _dot(h, wz_ref[...]).astype(BF16)
    for c in range(SSD_XBC // 512):
        u = _dot(hcat, wx_ref[:, c * 512:(c + 1) * 512])
        for s in range(4):
            u_ref[4 * c + s] = u[:, s * 128:(s + 1) * 128]
    for s in range(SSD_XBC // 128):
        sl = slice(s * 128, (s + 1) * 128)
        acc = cb_ref[:, sl] + cw_ref[0:1, sl] * u_ref[s, HALO - 2:HALO - 2 + tm, :]
        for k in range(1, SSD_CONV):
            acc = acc + cw_ref[k:k + 1, sl] * u_ref[s, HALO - 2 + k:HALO - 2 + k + tm, :]
        ssd_ref[:, SSD_WIDTH + s * 128:SSD_WIDTH + (s + 1) * 128] = _silu(acc).astype(BF16)

    dt_t = _dot_nt(wdt_ref[...], h)
    for k in range(tm // CHUNK):
        dt_ref[k] = dt_t[:, k * CHUNK:(k + 1) * CHUNK]

    cos = rope_ref[:, 0:128]
    sin = rope_ref[:, 128:256]
    lane = lax.broadcasted_iota(jnp.int32, (tm, 128), 1)
    first_half = (lane & (RET_QK - 1)) < RET_QK // 2

    def rotary(t, scale):
        outs = []
        for j in range(RET_QK_W // 128):
            tj = t[:, j * 128:(j + 1) * 128]
            partner = jnp.where(first_half, pltpu.roll(tj, 128 - RET_QK // 2, axis=1),
                                pltpu.roll(tj, RET_QK // 2, axis=1))
            outs.append((tj * cos + partner * sin) * scale)
        return jnp.concatenate(outs, axis=1)

    ret_ref[:, 0:RET_QK_W] = rotary(_dot(h, wr_ref[:, 0:RET_QK_W]), 1.0).astype(BF16)
    ret_ref[:, RET_QK_W:2 * RET_QK_W] = rotary(_dot(h, wr_ref[:, RET_QK_W:2 * RET_QK_W]),
                                              RET_QK ** -0.5).astype(BF16)
    for c in range(2):
        sl = slice(2 * RET_QK_W + c * 512, 2 * RET_QK_W + (c + 1) * 512)
        ret_ref[:, sl] = _dot(h, wr_ref[:, sl]).astype(BF16)

    for c in range(NA_IN_W // 512):
        sl = slice(c * 512, (c + 1) * 512)
        na_ref[:, sl] = _dot(h, wn_ref[:, sl]).astype(BF16)


def _halo_specs(tm, n_tok):
    r = tm // HALO
    last = n_tok // HALO - 1
    prev = pl.BlockSpec((HALO, D_MODEL), lambda i: (jnp.maximum(i * r - 1, 0), 0))
    nxt = pl.BlockSpec((HALO, D_MODEL), lambda i: (jnp.minimum((i + 1) * r, last), 0))
    return prev, nxt


def _proj_call(x, lw, rope, segs, tm):
    n_tok = x.shape[0]

    def rope_map(i):
        pos0, _ = _tile_pos(i, tm, segs)
        return (pos0 // tm, 0)

    prev_spec, next_spec = _halo_specs(tm, n_tok)
    tok = lambda w: pl.BlockSpec((tm, w), lambda i: (i, 0))
    return pl.pallas_call(
        functools.partial(_proj_kernel, segs, tm),
        grid=(n_tok // tm,),
        in_specs=[tok(D_MODEL), prev_spec, next_spec,
                  _const_spec((1, D_MODEL)),
                  _const_spec((D_MODEL, GATE_W)), _const_spec((1, GATE_W)),
                  _const_spec((D_MODEL, SSD_WIDTH)),
                  _const_spec((D_MODEL, SSD_XBC)), _const_spec((SSD_CONV, SSD_XBC)), _const_spec((1, SSD_XBC)),
                  _const_spec((2 * SSD_HEADS, D_MODEL)),
                  _const_spec((D_MODEL, RET_IN_W)),
                  pl.BlockSpec((tm, 256), rope_map),
                  _const_spec((D_MODEL, NA_IN_W))],
        out_specs=[tok(GATE_W), tok(SSD_IN_W),
                   pl.BlockSpec((tm // CHUNK, 2 * SSD_HEADS, CHUNK), lambda i: (i, 0, 0)),
                   tok(RET_IN_W), tok(NA_IN_W)],
        out_shape=[jax.ShapeDtypeStruct((n_tok, GATE_W), BF16),
                   jax.ShapeDtypeStruct((n_tok, SSD_IN_W), BF16),
                   jax.ShapeDtypeStruct((n_tok // CHUNK, 2 * SSD_HEADS, CHUNK), F32),
                   jax.ShapeDtypeStruct((n_tok, RET_IN_W), BF16),
                   jax.ShapeDtypeStruct((n_tok, NA_IN_W), BF16)],
        scratch_shapes=[pltpu.VMEM((SSD_XBC // 128, tm + 2 * HALO, 128), F32)],
        compiler_params=_params(("arbitrary",)),
        name="proj",
    )(x, x, x, lw["norm_mix"], lw["w_gate"], lw["gate_bias"], lw["w_z"], lw["w_xbc"], lw["conv_w"],
      lw["conv_b"], lw["w_dt"], lw["w_ret"], rope, lw["w_na"])


def _ssd_kernel(n_chunks, in_ref, dt_ref, dtb_ref, a_ref, dsk_ref, ng_ref, out_ref,
                sb_ref, sf_state, sb_state, exp_ref):
    q = CHUNK
    nh = SSD_HEADS
    gw = SSD_WIDTH // SSD_GROUPS
    ii = lax.broadcasted_iota(jnp.int32, (q, q), 0)
    jj = lax.broadcasted_iota(jnp.int32, (q, q), 1)
    causal = jj <= ii
    pre_m = (ii <= jj).astype(BF16)
    suf_m = (ii >= jj).astype(BF16)
    row16 = lax.broadcasted_iota(jnp.int32, (2 * nh, q), 0)
    is_fwd_row = row16 < nh
    lane_g = lax.broadcasted_iota(jnp.int32, (q, gw), 1)

    er = lax.broadcasted_iota(jnp.int32, (q, 4 * SSD_WIDTH), 0)
    ec = lax.broadcasted_iota(jnp.int32, (q, 4 * SSD_WIDTH), 1)
    ecc = ec & (2 * SSD_WIDTH - 1)
    hd = ((ecc >> 9) << 3) + ((ecc & (SSD_WIDTH - 1)) >> 6)
    exp_ref[...] = (er == jnp.where(ec < 2 * SSD_WIDTH, 2 * nh, 4 * nh) + hd).astype(BF16)

    dtb = dtb_ref[...]
    a_neg = a_ref[...]

    def prep(c):
        v = dt_ref[c] + dtb
        dtv = jnp.maximum(v, 0.0) + jnp.log1p(jnp.exp(-jnp.abs(v)))
        la = dtv * a_neg
        pre = _dot3(la, pre_m)
        suf = _dot3(la, suf_m)
        cum = jnp.where(is_fwd_row, pre, suf)
        tot = jnp.where(is_fwd_row[:, 0:1], pre[:, q - 1:q], suf[:, 0:1])
        wgt = jnp.exp(tot - cum) * dtv
        esc = jnp.exp(cum)
        pack = jnp.concatenate([cum, wgt, esc, jnp.zeros((q - 6 * nh, q), F32)], axis=0)
        return dtv, cum, pack.T

    def group_b_t(blk, gidx):
        b = blk[:, SSD_WIDTH + SSD_WIDTH + gidx * SSD_STATE:SSD_WIDTH + SSD_WIDTH + (gidx + 1) * SSD_STATE]
        return b.astype(F32).T.astype(BF16)

    def state_update(blk, xs, wexp, state_ref, decay_row):
        xw = (xs * wexp).astype(BF16)
        ds = [_dot(group_b_t(blk, gidx), xw[:, gidx * gw:(gidx + 1) * gw]) for gidx in range(SSD_GROUPS)]
        state_ref[...] = state_ref[...] * decay_row + jnp.concatenate(ds, axis=1)

    sb_state[...] = jnp.zeros_like(sb_state)

    def back_body(t, carry):
        c = n_chunks - 1 - t
        r0 = pl.multiple_of(c * q, q)
        blk = in_ref[pl.ds(r0, q), :]
        sb_ref[c] = sb_state[...].astype(BF16)
        _, _, pt = prep(c)
        ptb = pt.astype(BF16)
        w_b = _dot(ptb, exp_ref[:, SSD_WIDTH:2 * SSD_WIDTH])
        e_b = _dot(ptb, exp_ref[:, 3 * SSD_WIDTH:4 * SSD_WIDTH])
        xs = blk[:, SSD_WIDTH:2 * SSD_WIDTH].astype(F32)
        state_update(blk, xs, w_b, sb_state, e_b[0:1, :])
        return carry

    lax.fori_loop(0, n_chunks, back_body, 0)

    sf_state[...] = jnp.zeros_like(sf_state)

    def fwd_body(c, carry):
        r0 = pl.multiple_of(c * q, q)
        blk = in_ref[pl.ds(r0, q), :]
        z = blk[:, 0:SSD_WIDTH].astype(F32)
        xs_b = blk[:, SSD_WIDTH:2 * SSD_WIDTH]
        xs = xs_b.astype(F32)
        dtv, cum, pt = prep(c)
        we = _dot(pt.astype(BF16), exp_ref[...])
        w_f = we[:, 0:SSD_WIDTH]
        e_f = we[:, 2 * SSD_WIDTH:3 * SSD_WIDTH]
        e_b = we[:, 3 * SSD_WIDTH:4 * SSD_WIDTH]

        ys = []
        for gidx in range(SSD_GROUPS):
            c_off = SSD_WIDTH + SSD_WIDTH + SSD_BC + gidx * SSD_STATE
            b_off = SSD_WIDTH + SSD_WIDTH + gidx * SSD_STATE
            cg = blk[:, c_off:c_off + SSD_STATE]
            bg = blk[:, b_off:b_off + SSD_STATE]
            scores = _dot_nt(cg, bg)
            xg = xs_b[:, gidx * gw:(gidx + 1) * gw]
            yg = jnp.zeros((q, gw), F32)
            for r in range(SSD_HEADS // SSD_GROUPS):
                hf = gidx * (SSD_HEADS // SSD_GROUPS) + r
                hb = nh + hf
                seg_f = jnp.broadcast_to(pt[:, hf:hf + 1], (q, q)) - jnp.broadcast_to(cum[hf:hf + 1, :], (q, q))
                seg_b = jnp.broadcast_to(pt[:, hb:hb + 1], (q, q)) - jnp.broadcast_to(cum[hb:hb + 1, :], (q, q))
                seg = jnp.where(causal, seg_f, seg_b)
                dcol = jnp.where(causal, jnp.broadcast_to(dtv[hf:hf + 1, :], (q, q)),
                                 jnp.broadcast_to(dtv[hb:hb + 1, :], (q, q)))
                m = (scores * jnp.exp(seg) * dcol).astype(BF16)
                yr = _dot(m, xg)
                yg = jnp.where((lane_g >> 6) == r, yr, yg)
            sl = slice(gidx * gw, (gidx + 1) * gw)
            y_inter = (_dot(cg, sf_state[:, sl].astype(BF16)) * e_f[:, sl]
                       + _dot(cg, sb_ref[c][:, sl]) * e_b[:, sl])
            ys.append(yg + y_inter)
        y = jnp.concatenate(ys, axis=1) + dsk_ref[...] * xs
        o = y * _silu(z)
        o = o * lax.rsqrt(jnp.mean(o * o, axis=-1, keepdims=True) + EPS) * ng_ref[...]
        out_ref[pl.ds(r0, q), :] = o.astype(BF16)
        state_update(blk, xs, w_f, sf_state, e_f[q - 1:q, :])
        return carry

    lax.fori_loop(0, n_chunks, fwd_body, 0)


def _seq_call(kernel_fn, name, inputs, in_specs, out_width, n_tok, n_seq, seq_len, blk0, scratch, prev_out):
    out_spec = pl.BlockSpec((seq_len, out_width), lambda s: (blk0 + s, 0))
    kwargs = {}
    if prev_out is not None:
        inputs = list(inputs) + [prev_out]
        in_specs = list(in_specs) + [pl.BlockSpec(memory_space=pl.ANY)]
        kwargs["input_output_aliases"] = {len(inputs) - 1: 0}
        body = kernel_fn
        kernel_fn = lambda *refs: body(*refs[:len(inputs) - 1], *refs[len(inputs):])
    return pl.pallas_call(
        kernel_fn,
        grid=(n_seq,),
        in_specs=in_specs,
        out_specs=out_spec,
        out_shape=jax.ShapeDtypeStruct((n_tok, out_width), BF16),
        scratch_shapes=scratch,
        compiler_params=_params(("arbitrary",)),
        name=name,
        **kwargs,
    )(*inputs)


def _over_segments(segs, make_call):
    out = None
    base = 0
    for idx, (n_seq, seq_len) in enumerate(segs):
        assert base % seq_len == 0
        out = make_call(idx, n_seq, seq_len, base // seq_len, out)
        base += n_seq * seq_len
    return out


def _ssd_call(ssd_in, dt_raw, lw, segs):
    n_tok = ssd_in.shape[0]

    def make(idx, n_seq, seq_len, blk0, prev_out):
        n_chunks = seq_len // CHUNK
        in_specs = [pl.BlockSpec((seq_len, SSD_IN_W), lambda s: (blk0 + s, 0)),
                    pl.BlockSpec((n_chunks, 2 * SSD_HEADS, CHUNK), lambda s: (blk0 + s, 0, 0)),
                    _const_spec((2 * SSD_HEADS, 1)), _const_spec((2 * SSD_HEADS, 1)),
                    _const_spec((1, SSD_WIDTH)), _const_spec((1, SSD_WIDTH))]
        scratch = [pltpu.VMEM((n_chunks, SSD_STATE, SSD_WIDTH), BF16),
                   pltpu.VMEM((SSD_STATE, SSD_WIDTH), F32),
                   pltpu.VMEM((SSD_STATE, SSD_WIDTH), F32),
                   pltpu.VMEM((CHUNK, 4 * SSD_WIDTH), BF16)]
        return _seq_call(functools.partial(_ssd_kernel, n_chunks), f"ssd{idx}",
                         [ssd_in, dt_raw, lw["dt_bias"], lw["a_neg"], lw["d_skip"], lw["ssd_norm"]],
                         in_specs, SSD_WIDTH, n_tok, n_seq, seq_len, blk0, scratch, prev_out)

    return _over_segments(segs, make)


def _ret_kernel(n_chunks, in_ref, thq_ref, thv_ref, out_ref, sb_ref, sf_state, sb_state):
    q = CHUNK
    nh = RET_HEADS
    lgq = -jnp.exp(thq_ref[...])
    lgv = -jnp.exp(thv_ref[...])
    ii = lax.broadcasted_iota(jnp.int32, (q, q), 0)
    jj = lax.broadcasted_iota(jnp.int32, (q, q), 1)
    causal = jj <= ii
    dist = (ii - jj).astype(F32)
    tok = lax.broadcasted_iota(jnp.int32, (q, 1), 0).astype(F32)

    w_f = jnp.exp((q - 1.0 - tok) * lgq[0:1, :])
    w_b = jnp.exp(tok * lgq[1:2, :])
    e_f = jnp.exp((tok + 1.0) * lgv[0:1, :])
    e_b = jnp.exp((q - tok) * lgv[1:2, :])
    decays = []
    dec_f, dec_b = [], []
    for h in range(nh):
        lf = lgv[0:1, h * RET_V:(h + 1) * RET_V]
        lb = lgv[1:2, h * RET_V:(h + 1) * RET_V]
        decays.append(jnp.where(causal, jnp.exp(dist * lf), jnp.exp(-dist * lb)))
        dec_f.append(jnp.broadcast_to(jnp.exp(float(q) * lf), (RET_QK, RET_V)))
        dec_b.append(jnp.broadcast_to(jnp.exp(float(q) * lb), (RET_QK, RET_V)))
    dec_f = jnp.concatenate(dec_f, axis=0)
    dec_b = jnp.concatenate(dec_b, axis=0)
    lane_q = lax.broadcasted_iota(jnp.int32, (q, RET_QK_W), 1) >> 6

    def state_update(k, v, wrow, state_ref, dec):
        kw = (k.astype(F32) * wrow).T.astype(BF16)
        full = _dot(kw, v)
        ds = jnp.concatenate([full[h * RET_QK:(h + 1) * RET_QK, h * RET_V:(h + 1) * RET_V] for h in range(nh)],
                             axis=0)
        state_ref[...] = state_ref[...] * dec + ds

    sb_state[...] = jnp.zeros_like(sb_state)

    def back_body(t, carry):
        c = n_chunks - 1 - t
        r0 = pl.multiple_of(c * q, q)
        blk = in_ref[pl.ds(r0, q), :]
        sb_ref[c] = sb_state[...].astype(BF16)
        state_update(blk[:, RET_QK_W:2 * RET_QK_W], blk[:, 2 * RET_QK_W:2 * RET_QK_W + RET_WIDTH],
                     w_b, sb_state, dec_b)
        return carry

    lax.fori_loop(0, n_chunks, back_body, 0)

    sf_state[...] = jnp.zeros_like(sf_state)

    def fwd_body(c, carry):
        r0 = pl.multiple_of(c * q, q)
        blk = in_ref[pl.ds(r0, q), :]
        qv = blk[:, 0:RET_QK_W]
        kv = blk[:, RET_QK_W:2 * RET_QK_W]
        vv = blk[:, 2 * RET_QK_W:2 * RET_QK_W + RET_WIDTH]
        gv = blk[:, 2 * RET_QK_W + RET_WIDTH:RET_IN_W].astype(F32)
        zero = jnp.zeros_like(qv)
        k_bd = jnp.concatenate([jnp.where(lane_q == h, kv, zero) for h in range(nh)], axis=0)
        s_all = _dot_nt(qv, k_bd)
        sf_b = sf_state[...].astype(BF16)
        sb_b = sb_ref[c]
        outs = []
        for h in range(nh):
            sl = slice(h * RET_V, (h + 1) * RET_V)
            p = (s_all[:, h * q:(h + 1) * q] * decays[h]).astype(BF16)
            qh = jnp.where(lane_q == h, qv, zero)
            y = (_dot(p, vv[:, sl]) + _dot(qh, sf_b) * e_f[:, sl] + _dot(qh, sb_b) * e_b[:, sl])
            outs.append(y * lax.rsqrt(jnp.mean(y * y, axis=-1, keepdims=True) + EPS))
        o = jnp.concatenate(outs, axis=1) * _silu(gv)
        out_ref[pl.ds(r0, q), :] = o.astype(BF16)
        state_update(kv, vv, w_f, sf_state, dec_f)
        return carry

    lax.fori_loop(0, n_chunks, fwd_body, 0)


def _ret_call(ret_in, lw, segs):
    n_tok = ret_in.shape[0]

    def make(idx, n_seq, seq_len, blk0, prev_out):
        n_chunks = seq_len // CHUNK
        in_specs = [pl.BlockSpec((seq_len, RET_IN_W), lambda s: (blk0 + s, 0)),
                    _const_spec((2, RET_QK_W)), _const_spec((2, RET_WIDTH))]
        scratch = [pltpu.VMEM((n_chunks, RET_QK_W, RET_V), BF16),
                   pltpu.VMEM((RET_QK_W, RET_V), F32),
                   pltpu.VMEM((RET_QK_W, RET_V), F32)]
        return _seq_call(functools.partial(_ret_kernel, n_chunks), f"ret{idx}",
                         [ret_in, lw["theta_q"], lw["theta_v"]],
                         in_specs, RET_WIDTH, n_tok, n_seq, seq_len, blk0, scratch, prev_out)

    return _over_segments(segs, make)


def _na_kernel(n_rows, in_ref, bias_ref, out_ref, s_ref):
    w = GRID_W
    n_pairs = NA_HEADS // 2
    lane = lax.broadcasted_iota(jnp.int32, (w, 128), 1)
    low = lane < NA_HEADDIM
    scale = NA_HEADDIM ** -0.5

    def window(r):
        r = jnp.asarray(r, jnp.int32)
        rs = jnp.clip(r - NA_ROWS // 2, 0, n_rows - NA_ROWS)
        return r - rs, pl.multiple_of(r * w, w), pl.multiple_of(rs * w, w)

    def logits_stage(r, slot):
        d, q0, k0 = window(r)
        for p in range(n_pairs):
            qp = in_ref[pl.ds(q0, w), p * 128:(p + 1) * 128] * scale
            zero = jnp.zeros_like(qp)
            lhs = jnp.concatenate([jnp.where(low, qp, zero), jnp.where(low, zero, qp)], axis=0)
            kp = in_ref[pl.ds(k0, NA_WIN), NA_WIDTH + p * 128:NA_WIDTH + (p + 1) * 128]
            s_ref[slot, p] = _dot_nt(lhs, kp) + bias_ref[d, p]

    def value_stage(r, slot):
        _, q0, k0 = window(r)
        for p in range(n_pairs):
            m = jnp.max(s_ref[slot, p], axis=-1, keepdims=True)
            e = jnp.exp(s_ref[slot, p] - m)
            l = jnp.sum(e, axis=-1, keepdims=True)
            vp = in_ref[pl.ds(k0, NA_WIN), 2 * NA_WIDTH + p * 128:2 * NA_WIDTH + (p + 1) * 128]
            o = _dot(e.astype(BF16), vp) / l
            out_ref[pl.ds(q0, w), p * 128:(p + 1) * 128] = jnp.where(low, o[0:w], o[w:2 * w]).astype(BF16)

    assert n_rows % 2 == 0
    logits_stage(0, 0)

    def row_body(t, carry):
        r = 2 * t
        logits_stage(r + 1, 1)
        value_stage(r, 0)
        logits_stage(r + 2, 0)
        value_stage(r + 1, 1)
        return carry

    lax.fori_loop(0, n_rows // 2 - 1, row_body, 0)
    logits_stage(n_rows - 1, 1)
    value_stage(n_rows - 2, 0)
    value_stage(n_rows - 1, 1)


def _na_call(na_in, bias, segs):
    n_tok = na_in.shape[0]

    def make(idx, n_seq, seq_len, blk0, prev_out):
        n_rows = seq_len // GRID_W
        assert n_rows >= NA_ROWS
        in_specs = [pl.BlockSpec((seq_len, NA_IN_W), lambda s: (blk0 + s, 0)),
                    _const_spec((NA_ROWS, NA_HEADS // 2, 2 * GRID_W, NA_WIN))]
        scratch = [pltpu.VMEM((2, NA_HEADS // 2, 2 * GRID_W, NA_WIN), F32)]
        return _seq_call(functools.partial(_na_kernel, n_rows), f"na{idx}", [na_in, bias],
                         in_specs, NA_WIDTH, n_tok, n_seq, seq_len, blk0, scratch, prev_out)

    return _over_segments(segs, make)


def _na_bias_table(rpb):
    cols = np.arange(GRID_W)
    col_start = np.clip(cols - NA_COLS // 2, 0, GRID_W - NA_COLS)
    kc = np.arange(GRID_W)
    valid = (kc[None, :] >= col_start[:, None]) & (kc[None, :] < col_start[:, None] + NA_COLS)
    col_off = np.clip(kc[None, :] - cols[:, None] + NA_COLS - 1, 0, 2 * NA_COLS - 2)
    toep = jnp.where(valid[None, None], rpb.astype(F32)[:, :, col_off], NEG_BIG)
    per_shift = []
    for d in range(NA_ROWS):
        rows = [toep[:, x - d + NA_ROWS - 1] for x in range(NA_ROWS)]
        per_shift.append(jnp.stack(rows, axis=2).reshape(NA_HEADS, GRID_W, NA_WIN))
    table = jnp.stack(per_shift, axis=0)
    return table.reshape(NA_ROWS, NA_HEADS // 2, 2 * GRID_W, NA_WIN)


def _merge_kernel(x_ref, gates_ref, o0_ref, o1_ref, o2_ref, wb_ref, wo_ref, out_ref):
    merged = None
    for b, o_ref in enumerate((o0_ref, o1_ref, o2_ref)):
        t = gates_ref[:, b * D_MODEL:(b + 1) * D_MODEL].astype(F32) * _dot(o_ref[...], wb_ref[b])
        merged = t if merged is None else merged + t
    out_ref[...] = x_ref[...] + _dot(merged.astype(BF16), wo_ref[...])


def _merge_call(x, gates, o_ssd, o_ret, o_na, lw, tm):
    n_tok = x.shape[0]
    tok = lambda w: pl.BlockSpec((tm, w), lambda i: (i, 0))
    return pl.pallas_call(
        _merge_kernel,
        grid=(n_tok // tm,),
        in_specs=[tok(D_MODEL), tok(GATE_W), tok(SSD_WIDTH), tok(RET_WIDTH), tok(NA_WIDTH),
                  _const_spec((N_BRANCH, SSD_WIDTH, D_MODEL)), _const_spec((D_MODEL, D_MODEL))],
        out_specs=tok(D_MODEL),
        out_shape=jax.ShapeDtypeStruct((n_tok, D_MODEL), F32),
        compiler_params=_params(("arbitrary",)),
        name="merge",
    )(x, gates, o_ssd, o_ret, o_na, lw["w_branch"], lw["w_out"])


def _ffn_kernel(segs, tm, final, x_ref, xp_ref, xn_ref, g_ref, wup_ref, cw_ref, cb_ref, wdn_ref, gf_ref,
                out_ref, h_ref, acc_ref, u_ref):
    i = pl.program_id(0)
    pos0, seq_len = _tile_pos(i, tm, segs)
    g = g_ref[...]

    def rms(v):
        return v * lax.rsqrt(jnp.mean(v * v, axis=-1, keepdims=True) + EPS) * g

    x = x_ref[...]
    prev_ok = (pos0 > 0).astype(F32)
    next_ok = (pos0 + tm < seq_len).astype(F32)
    h_ref[0:HALO, :] = (rms(xp_ref[...]) * prev_ok).astype(BF16)
    h_ref[HALO:HALO + tm, :] = rms(x).astype(BF16)
    h_ref[HALO + tm:2 * HALO + tm, :] = (rms(xn_ref[...]) * next_ok).astype(BF16)
    acc_ref[...] = jnp.zeros_like(acc_ref)

    n_slabs = FF_CHUNK // 128

    def up_stage(c, slot):
        hcat = h_ref[...]
        for j, cc in enumerate((c, N_FF_CHUNKS + c)):
            u = _dot(hcat, wup_ref[cc])
            for s in range(n_slabs):
                u_ref[slot, j, s] = u[:, s * 128:(s + 1) * 128]

    def conv(slot, j, cw, cb):
        outs = []
        for s in range(n_slabs):
            sl = slice(s * 128, (s + 1) * 128)
            acc = cb[:, sl] + cw[0:1, sl] * u_ref[slot, j, s, HALO - 1:HALO - 1 + tm, :]
            for k in range(1, FFN_CONV):
                acc = acc + cw[k:k + 1, sl] * u_ref[slot, j, s, HALO - 1 + k:HALO - 1 + k + tm, :]
            outs.append(acc)
        return jnp.concatenate(outs, axis=1)

    def down_stage(c, slot):
        val = conv(slot, 0, cw_ref[c], cb_ref[c])
        gate = conv(slot, 1, cw_ref[N_FF_CHUNKS + c], cb_ref[N_FF_CHUNKS + c])
        act = (_silu(gate) * val).astype(BF16)
        acc_ref[...] += _dot(act, wdn_ref[c])

    assert N_FF_CHUNKS % 2 == 1
    up_stage(0, 0)

    def body(t, carry):
        c = 2 * t
        up_stage(c + 1, 1)
        down_stage(c, 0)
        up_stage(c + 2, 0)
        down_stage(c + 1, 1)
        return carry

    lax.fori_loop(0, N_FF_CHUNKS // 2, body, 0)
    down_stage(N_FF_CHUNKS - 1, 0)
    y = x + acc_ref[...]
    if final:
        y = y * lax.rsqrt(jnp.mean(y * y, axis=-1, keepdims=True) + EPS) * gf_ref[...]
    out_ref[...] = y


def _ffn_call(x, lw, norm_final, segs, tm, final):
    n_tok = x.shape[0]
    prev_spec, next_spec = _halo_specs(tm, n_tok)
    tok = pl.BlockSpec((tm, D_MODEL), lambda i: (i, 0))
    return pl.pallas_call(
        functools.partial(_ffn_kernel, segs, tm, final),
        grid=(n_tok // tm,),
        in_specs=[tok, prev_spec, next_spec, _const_spec((1, D_MODEL)),
                  _const_spec((2 * N_FF_CHUNKS, D_MODEL, FF_CHUNK)),
                  _const_spec((2 * N_FF_CHUNKS, FFN_CONV, FF_CHUNK)),
                  _const_spec((2 * N_FF_CHUNKS, 1, FF_CHUNK)),
                  _const_spec((N_FF_CHUNKS, FF_CHUNK, D_MODEL)),
                  _const_spec((1, D_MODEL))],
        out_specs=tok,
        out_shape=jax.ShapeDtypeStruct((n_tok, D_MODEL), F32),
        scratch_shapes=[pltpu.VMEM((tm + 2 * HALO, D_MODEL), BF16), pltpu.VMEM((tm, D_MODEL), F32),
                        pltpu.VMEM((2, 2, FF_CHUNK // 128, tm + 2 * HALO, 128), F32)],
        compiler_params=_params(("arbitrary",)),
        name="ffn",
    )(x, x, x, lw["norm_ffn"], lw["w_up"], lw["ffn_conv_w"], lw["ffn_conv_b"], lw["w_down"], norm_final)


def _layer_weights(i, norm_mix, w_in, gate_bias, ssd_conv_w, ssd_conv_b, ssd_dt_bias, ssd_a_log, ssd_d, ssd_norm,
                   ret_theta, w_branch, w_out, norm_ffn, ffn_w_up, ffn_conv_w, ffn_conv_b, ffn_w_down):
    w = w_in[i]
    offs = np.cumsum([0, SSD_WIDTH, SSD_XBC, 2 * SSD_HEADS, RET_IN_W, NA_IN_W, GATE_W])
    col = lambda k: w[:, offs[k]:offs[k + 1]]
    up = ffn_w_up[i].astype(BF16).reshape(D_MODEL, 2 * N_FF_CHUNKS, FF_CHUNK).transpose(1, 0, 2)
    return {
        "norm_mix": norm_mix[i].reshape(1, D_MODEL),
        "w_z": col(0).astype(BF16),
        "w_xbc": col(1).astype(BF16),
        "w_dt": col(2).T.astype(BF16),
        "w_ret": col(3).astype(BF16),
        "w_na": col(4).astype(BF16),
        "w_gate": col(5).astype(BF16),
        "gate_bias": gate_bias[i].reshape(1, GATE_W),
        "conv_w": ssd_conv_w[i],
        "conv_b": ssd_conv_b[i].reshape(1, SSD_XBC),
        "dt_bias": ssd_dt_bias[i].reshape(2 * SSD_HEADS, 1),
        "a_neg": -jnp.exp(ssd_a_log[i].astype(F32)).reshape(2 * SSD_HEADS, 1),
        "d_skip": jnp.repeat(ssd_d[i], SSD_HEADDIM).reshape(1, SSD_WIDTH),
        "ssd_norm": ssd_norm[i].reshape(1, SSD_WIDTH),
        "theta_q": jnp.repeat(ret_theta[i], RET_QK, axis=1),
        "theta_v": jnp.repeat(ret_theta[i], RET_V, axis=1),
        "w_branch": w_branch[i].astype(BF16),
        "w_out": w_out[i].astype(BF16),
        "norm_ffn": norm_ffn[i].reshape(1, D_MODEL),
        "w_up": up,
        "ffn_conv_w": ffn_conv_w[i].reshape(FFN_CONV, 2 * N_FF_CHUNKS, FF_CHUNK).transpose(1, 0, 2),
        "ffn_conv_b": ffn_conv_b[i].reshape(2 * N_FF_CHUNKS, 1, FF_CHUNK),
        "w_down": ffn_w_down[i].astype(BF16).reshape(N_FF_CHUNKS, FF_CHUNK, D_MODEL),
    }


def _rope_table(max_len):
    half = RET_QK // 2
    inv = 1.0 / (ROPE_BASE ** (jnp.arange(half, dtype=F32) / half))
    ang = jnp.arange(max_len, dtype=F32)[:, None] * inv[None, :]
    cos = jnp.tile(jnp.cos(ang), (1, 128 // half))
    sin = jnp.tile(jnp.concatenate([-jnp.sin(ang), jnp.sin(ang)], axis=1), (1, 128 // RET_QK))
    return jnp.concatenate([cos, sin], axis=1)


def _pick_tile(segs):
    tm = 512
    while any(seq_len % tm for _, seq_len in segs):
        tm //= 2
    assert tm >= CHUNK
    return tm


def _trunk(xs, norm_mix, w_in, gate_bias, ssd_conv_w, ssd_conv_b, ssd_dt_bias, ssd_a_log, ssd_d, ssd_norm,
           ret_theta, na_rpb, w_branch, w_out, norm_ffn, ffn_w_up, ffn_conv_w, ffn_conv_b, ffn_w_down, norm_final):
    segs = tuple((int(x.shape[0]), int(x.shape[1])) for x in xs)
    tm = _pick_tile(segs)
    x = jnp.concatenate([x.reshape(-1, D_MODEL) for x in xs], axis=0)
    rope = _rope_table(max(seq_len for _, seq_len in segs))
    depth = w_in.shape[0]
    gf = norm_final.reshape(1, D_MODEL)
    for i in range(depth):
        lw = _layer_weights(i, norm_mix, w_in, gate_bias, ssd_conv_w, ssd_conv_b, ssd_dt_bias, ssd_a_log, ssd_d,
                            ssd_norm, ret_theta, w_branch, w_out, norm_ffn, ffn_w_up, ffn_conv_w, ffn_conv_b,
                            ffn_w_down)
        gates, ssd_in, dt_raw, ret_in, na_in = _proj_call(x, lw, rope, segs, tm)
        o_ssd = _ssd_call(ssd_in, dt_raw, lw, segs)
        o_ret = _ret_call(ret_in, lw, segs)
        o_na = _na_call(na_in, _na_bias_table(na_rpb[i]), segs)
        x = _merge_call(x, gates, o_ssd, o_ret, o_na, lw, tm)
        x = _ffn_call(x, lw, gf, segs, tm, final=(i == depth - 1))
    outs = []
    base = 0
    for (n_seq, seq_len), x_in in zip(segs, xs):
        outs.append(x[base:base + n_seq * seq_len].reshape(x_in.shape))
        base += n_seq * seq_len
    return tuple(outs)


def kernel(x_prompt, x_sample, norm_mix, w_in, gate_bias, ssd_conv_w, ssd_conv_b, ssd_dt_bias, ssd_a_log, ssd_d,
           ssd_norm, ret_theta, na_rpb, w_branch, w_out, norm_ffn, ffn_w_up, ffn_conv_w, ffn_conv_b, ffn_w_down,
           norm_final):
    return _trunk((x_prompt, x_sample), norm_mix, w_in, gate_bias, ssd_conv_w, ssd_conv_b, ssd_dt_bias, ssd_a_log,
                  ssd_d, ssd_norm, ret_theta, na_rpb, w_branch, w_out, norm_ffn, ffn_w_up, ffn_conv_w, ffn_conv_b,
                  ffn_w_down, norm_final)
```

```python
import functools

import jax
import jax.numpy as jnp
import numpy as np
from jax import lax
from jax.experimental import pallas as pl
from jax.experimental.pallas import tpu as pltpu

F32 = jnp.float32
BF16 = jnp.bfloat16

D_MODEL = 1024
GRID_W = 64
CHUNK = 128
SSD_HEADS = 8
SSD_HEADDIM = 64
SSD_WIDTH = SSD_HEADS * SSD_HEADDIM
SSD_GROUPS = 2
SSD_STATE = 128
SSD_BC = SSD_GROUPS * SSD_STATE
SSD_XBC = SSD_WIDTH + 2 * SSD_BC
SSD_CONV = 5
RET_HEADS = 4
RET_QK = 64
RET_V = 128
RET_WIDTH = RET_HEADS * RET_V
ROPE_BASE = 10000.0
NA_HEADS = 8
NA_HEADDIM = 64
NA_WIDTH = NA_HEADS * NA_HEADDIM
NA_ROWS = 8
NA_COLS = 16
N_BRANCH = 3
D_FF = 2816
FFN_CONV = 3
EPS = 1e-6

GATE_W = N_BRANCH * D_MODEL
SSD_IN_W = SSD_WIDTH + SSD_XBC
RET_QK_W = RET_HEADS * RET_QK
RET_IN_W = 2 * RET_QK_W + 2 * RET_WIDTH
NA_IN_W = 3 * NA_WIDTH
NA_WIN = NA_ROWS * GRID_W
HALO = 16
FF_CHUNK = 256
N_FF_CHUNKS = D_FF // FF_CHUNK
NEG_BIG = -1e30
VMEM_LIMIT = 56 * 1024 * 1024
SCAN_UNROLL = 4


def _sigmoid(v):
    return 1.0 / (1.0 + jnp.exp(-v))


def _silu(v):
    return v * _sigmoid(v)


def _dot(a, b):
    return jnp.dot(a, b, preferred_element_type=F32)


def _dot_nt(a, b):
    return lax.dot_general(a, b, (((1,), (1,)), ((), ())), preferred_element_type=F32)


def _dot3(x, m01):
    x1 = x.astype(BF16)
    r1 = x - x1.astype(F32)
    x2 = r1.astype(BF16)
    x3 = (r1 - x2.astype(F32)).astype(BF16)
    return _dot(x1, m01) + _dot(x2, m01) + _dot(x3, m01)


def _tile_pos(i, tm, segs):
    t0 = i * tm
    pos = None
    length = None
    base = 0
    for n_seq, seq_len in segs:
        p = lax.rem(t0 - base, seq_len)
        if pos is None:
            pos, length = p, jnp.int32(seq_len)
        else:
            pos = jnp.where(t0 >= base, p, pos)
            length = jnp.where(t0 >= base, jnp.int32(seq_len), length)
        base += n_seq * seq_len
    return pos, length


def _const_spec(shape):
    nd = len(shape)
    return pl.BlockSpec(shape, lambda *_: (0,) * nd, pipeline_mode=pl.Buffered(1))


def _params(sem):
    return pltpu.CompilerParams(dimension_semantics=sem, vmem_limit_bytes=VMEM_LIMIT)


def _proj_kernel(segs, tm, x_ref, xp_ref, xn_ref, g_ref, wg_ref, gb_ref, wz_ref, wx_ref, cw_ref, cb_ref,
                 wdt_ref, wr_ref, rope_ref, wn_ref, gates_ref, ssd_ref, dt_ref, ret_ref, na_ref, u_ref):
    i = pl.program_id(0)
    pos0, seq_len = _tile_pos(i, tm, segs)
    g = g_ref[...]

    def rms(v):
        return v * lax.rsqrt(jnp.mean(v * v, axis=-1, keepdims=True) + EPS) * g

    h = rms(x_ref[...]).astype(BF16)
    prev_ok = (pos0 > 0).astype(F32)
    next_ok = (pos0 + tm < seq_len).astype(F32)
    hp = (rms(xp_ref[...]) * prev_ok).astype(BF16)
    hn = (rms(xn_ref[...]) * next_ok).astype(BF16)
    hcat = jnp.concatenate([hp, h, hn], axis=0)

    for c in range(GATE_W // 512):
        sl = slice(c * 512, (c + 1) * 512)
        a = _dot(h, wg_ref[:, sl]) + gb_ref[:, sl]
        gates_ref[:, sl] = _sigmoid(a).astype(BF16)

    ssd_ref[:, 0:SSD_WIDTH] = _dot(h, wz_ref[...]).astype(BF16)
    for c in range(SSD_XBC // 512):
        u = _dot(hcat, wx_ref[:, c * 512:(c + 1) * 512])
        for s in range(4):
            u_ref[4 * c + s] = u[:, s * 128:(s + 1) * 128]
    for s in range(SSD_XBC // 128):
        sl = slice(s * 128, (s + 1) * 128)
        acc = cb_ref[:, sl] + cw_ref[0:1, sl] * u_ref[s, HALO - 2:HALO - 2 + tm, :]
        for k in range(1, SSD_CONV):
            acc = acc + cw_ref[k:k + 1, sl] * u_ref[s, HALO - 2 + k:HALO - 2 + k + tm, :]
        ssd_ref[:, SSD_WIDTH + s * 128:SSD_WIDTH + (s + 1) * 128] = _silu(acc).astype(BF16)

    dt_t = _dot_nt(wdt_ref[...], h)
    for k in range(tm // CHUNK):
        dt_ref[k] = dt_t[:, k * CHUNK:(k + 1) * CHUNK]

    cos = rope_ref[:, 0:128]
    sin = rope_ref[:, 128:256]
    lane = lax.broadcasted_iota(jnp.int32, (tm, 128), 1)
    first_half = (lane & (RET_QK - 1)) < RET_QK // 2

    def rotary(t, scale):
        outs = []
        for j in range(RET_QK_W // 128):
            tj = t[:, j * 128:(j + 1) * 128]
            partner = jnp.where(first_half, pltpu.roll(tj, 128 - RET_QK // 2, axis=1),
                                pltpu.roll(tj, RET_QK // 2, axis=1))
            outs.append((tj * cos + partner * sin) * scale)
        return jnp.concatenate(outs, axis=1)

    ret_ref[:, 0:RET_QK_W] = rotary(_dot(h, wr_ref[:, 0:RET_QK_W]), 1.0).astype(BF16)
    ret_ref[:, RET_QK_W:2 * RET_QK_W] = rotary(_dot(h, wr_ref[:, RET_QK_W:2 * RET_QK_W]),
                                              RET_QK ** -0.5).astype(BF16)
    for c in range(2):
        sl = slice(2 * RET_QK_W + c * 512, 2 * RET_QK_W + (c + 1) * 512)
        ret_ref[:, sl] = _dot(h, wr_ref[:, sl]).astype(BF16)

    for c in range(NA_IN_W // 512):
        sl = slice(c * 512, (c + 1) * 512)
        na_ref[:, sl] = _dot(h, wn_ref[:, sl]).astype(BF16)


def _halo_specs(tm, n_tok):
    r = tm // HALO
    last = n_tok // HALO - 1
    prev = pl.BlockSpec((HALO, D_MODEL), lambda i: (jnp.maximum(i * r - 1, 0), 0))
    nxt = pl.BlockSpec((HALO, D_MODEL), lambda i: (jnp.minimum((i + 1) * r, last), 0))
    return prev, nxt


def _proj_call(x, lw, rope, segs, tm):
    n_tok = x.shape[0]

    def rope_map(i):
        pos0, _ = _tile_pos(i, tm, segs)
        return (pos0 // tm, 0)

    prev_spec, next_spec = _halo_specs(tm, n_tok)
    tok = lambda w: pl.BlockSpec((tm, w), lambda i: (i, 0))
    return pl.pallas_call(
        functools.partial(_proj_kernel, segs, tm),
        grid=(n_tok // tm,),
        in_specs=[tok(D_MODEL), prev_spec, next_spec,
                  _const_spec((1, D_MODEL)),
                  _const_spec((D_MODEL, GATE_W)), _const_spec((1, GATE_W)),
                  _const_spec((D_MODEL, SSD_WIDTH)),
                  _const_spec((D_MODEL, SSD_XBC)), _const_spec((SSD_CONV, SSD_XBC)), _const_spec((1, SSD_XBC)),
                  _const_spec((2 * SSD_HEADS, D_MODEL)),
                  _const_spec((D_MODEL, RET_IN_W)),
                  pl.BlockSpec((tm, 256), rope_map),
                  _const_spec((D_MODEL, NA_IN_W))],
        out_specs=[tok(GATE_W), tok(SSD_IN_W),
                   pl.BlockSpec((tm // CHUNK, 2 * SSD_HEADS, CHUNK), lambda i: (i, 0, 0)),
                   tok(RET_IN_W), tok(NA_IN_W)],
        out_shape=[jax.ShapeDtypeStruct((n_tok, GATE_W), BF16),
                   jax.ShapeDtypeStruct((n_tok, SSD_IN_W), BF16),
                   jax.ShapeDtypeStruct((n_tok // CHUNK, 2 * SSD_HEADS, CHUNK), F32),
                   jax.ShapeDtypeStruct((n_tok, RET_IN_W), BF16),
                   jax.ShapeDtypeStruct((n_tok, NA_IN_W), BF16)],
        scratch_shapes=[pltpu.VMEM((SSD_XBC // 128, tm + 2 * HALO, 128), F32)],
        compiler_params=_params(("arbitrary",)),
        name="proj",
    )(x, x, x, lw["norm_mix"], lw["w_gate"], lw["gate_bias"], lw["w_z"], lw["w_xbc"], lw["conv_w"],
      lw["conv_b"], lw["w_dt"], lw["w_ret"], rope, lw["w_na"])


def _mixer_block(segs):
    blk = max(seq_len for _, seq_len in segs)
    base = 0
    for n_seq, seq_len in segs:
        assert blk % seq_len == 0 and base % blk == 0 and (n_seq * seq_len) % blk == 0
        assert seq_len % (SCAN_UNROLL * CHUNK) == 0 and seq_len // GRID_W >= NA_ROWS
        base += n_seq * seq_len
    return blk


def _sub_sequences(segs, blk):
    _, seq_len = _tile_pos(pl.program_id(0), blk, segs)
    return seq_len, blk // seq_len


def _mixer_call(kernel_fn, name, inputs, in_specs, out_width, n_tok, blk, scratch):
    return pl.pallas_call(
        kernel_fn,
        grid=(n_tok // blk,),
        in_specs=in_specs,
        out_specs=pl.BlockSpec((blk, out_width), lambda s: (s, 0)),
        out_shape=jax.ShapeDtypeStruct((n_tok, out_width), BF16),
        scratch_shapes=scratch,
        compiler_params=_params(("arbitrary",)),
        name=name,
    )(*inputs)


def _ssd_kernel(segs, blk, in_ref, dt_ref, dtb_ref, a_ref, dsk_ref, ng_ref, out_ref,
                cadj_ref, pt_ref, sf_in_ref, sb_in_ref, sf_state, sb_state, exp_ref):
    q = CHUNK
    nh = SSD_HEADS
    gw = SSD_WIDTH // SSD_GROUPS
    seq_len, n_sub = _sub_sequences(segs, blk)
    n_chunks = seq_len // q
    ii = lax.broadcasted_iota(jnp.int32, (q, q), 0)
    jj = lax.broadcasted_iota(jnp.int32, (q, q), 1)
    causal = jj <= ii
    pre_m = (ii <= jj).astype(BF16)
    suf_m = (ii >= jj).astype(BF16)
    row16 = lax.broadcasted_iota(jnp.int32, (2 * nh, q), 0)
    is_fwd_row = row16 < nh
    lane_g = lax.broadcasted_iota(jnp.int32, (q, gw), 1)

    er = lax.broadcasted_iota(jnp.int32, (q, 4 * SSD_WIDTH), 0)
    ec = lax.broadcasted_iota(jnp.int32, (q, 4 * SSD_WIDTH), 1)
    ecc = ec & (2 * SSD_WIDTH - 1)
    hd = ((ecc >> 9) << 3) + ((ecc & (SSD_WIDTH - 1)) >> 6)
    exp_ref[...] = (er == jnp.where(ec < 2 * SSD_WIDTH, 2 * nh, 4 * nh) + hd).astype(BF16)

    dtb = dtb_ref[...]
    a_neg = a_ref[...]

    def prep(c):
        v = dt_ref[c] + dtb
        dtv = jnp.maximum(v, 0.0) + jnp.log1p(jnp.exp(-jnp.abs(v)))
        la = dtv * a_neg
        pre = _dot3(la, pre_m)
        suf = _dot3(la, suf_m)
        cum = jnp.where(is_fwd_row, pre, suf)
        tot = jnp.where(is_fwd_row[:, 0:1], pre[:, q - 1:q], suf[:, 0:1])
        wgt = jnp.exp(tot - cum) * dtv
        esc = jnp.exp(cum)
        pack = jnp.concatenate([cum, wgt, esc, jnp.zeros((q - 6 * nh, q), F32)], axis=0)
        cadj_ref[c] = cum - jnp.log(dtv)
        pt_ref[c] = pack.T

    def scan_step(c, fwd):
        state_ref, rec_ref = (sf_state, sf_in_ref) if fwd else (sb_state, sb_in_ref)
        w_cols = slice(0, SSD_WIDTH) if fwd else slice(SSD_WIDTH, 2 * SSD_WIDTH)
        e_cols = slice(2 * SSD_WIDTH, 3 * SSD_WIDTH) if fwd else slice(3 * SSD_WIDTH, 4 * SSD_WIDTH)
        rows = in_ref[pl.ds(pl.multiple_of(c * q, q), q), :]
        ptb = pt_ref[c].astype(BF16)
        w = _dot(ptb, exp_ref[:, w_cols])
        edge = _dot(ptb[q - HALO:q] if fwd else ptb[0:HALO], exp_ref[:, e_cols])
        decay_row = edge[HALO - 1:HALO] if fwd else edge[0:1]
        rec_ref[c] = state_ref[...].astype(BF16)
        xw = (rows[:, SSD_WIDTH:2 * SSD_WIDTH].astype(F32) * w).astype(BF16)
        ds = []
        for gidx in range(SSD_GROUPS):
            b_off = 2 * SSD_WIDTH + gidx * SSD_STATE
            b_t = rows[:, b_off:b_off + SSD_STATE].astype(F32).T.astype(BF16)
            ds.append(_dot(b_t, xw[:, gidx * gw:(gidx + 1) * gw]))
        state_ref[...] = state_ref[...] * decay_row + jnp.concatenate(ds, axis=1)

    def outputs(c):
        r0 = pl.multiple_of(c * q, q)
        rows = in_ref[pl.ds(r0, q), :]
        z = rows[:, 0:SSD_WIDTH].astype(F32)
        xs_b = rows[:, SSD_WIDTH:2 * SSD_WIDTH]
        xs = xs_b.astype(F32)
        cadj = cadj_ref[c]
        pt = pt_ref[c]
        e_fb = _dot(pt.astype(BF16), exp_ref[:, 2 * SSD_WIDTH:4 * SSD_WIDTH])
        e_f = e_fb[:, 0:SSD_WIDTH]
        e_b = e_fb[:, SSD_WIDTH:2 * SSD_WIDTH]
        sf_in = sf_in_ref[c]
        sb_in = sb_in_ref[c]
        ys = []
        for gidx in range(SSD_GROUPS):
            c_off = 2 * SSD_WIDTH + SSD_BC + gidx * SSD_STATE
            b_off = 2 * SSD_WIDTH + gidx * SSD_STATE
            cg = rows[:, c_off:c_off + SSD_STATE]
            bg = rows[:, b_off:b_off + SSD_STATE]
            scores = _dot_nt(cg, bg)
            xg = xs_b[:, gidx * gw:(gidx + 1) * gw]
            yg = jnp.zeros((q, gw), F32)
            for r in range(SSD_HEADS // SSD_GROUPS):
                hf = gidx * (SSD_HEADS // SSD_GROUPS) + r
                hb = nh + hf
                seg_f = jnp.broadcast_to(pt[:, hf:hf + 1], (q, q)) - jnp.broadcast_to(cadj[hf:hf + 1, :], (q, q))
                seg_b = jnp.broadcast_to(pt[:, hb:hb + 1], (q, q)) - jnp.broadcast_to(cadj[hb:hb + 1, :], (q, q))
                m = (scores * jnp.exp(jnp.where(causal, seg_f, seg_b))).astype(BF16)
                yr = _dot(m, xg)
                yg = jnp.where((lane_g >> 6) == r, yr, yg)
            sl = slice(gidx * gw, (gidx + 1) * gw)
            y_inter = _dot(cg, sf_in[:, sl]) * e_f[:, sl] + _dot(cg, sb_in[:, sl]) * e_b[:, sl]
            ys.append(yg + y_inter)
        y = jnp.concatenate(ys, axis=1) + dsk_ref[...] * xs
        o = y * _silu(z)
        o = o * lax.rsqrt(jnp.mean(o * o, axis=-1, keepdims=True) + EPS) * ng_ref[...]
        out_ref[pl.ds(r0, q), :] = o.astype(BF16)

    def sub_body(sub, carry):
        c0 = sub * n_chunks

        def prep_body(t, cy):
            for u in range(SCAN_UNROLL):
                prep(c0 + SCAN_UNROLL * t + u)
            return cy

        lax.fori_loop(0, n_chunks // SCAN_UNROLL, prep_body, 0)

        sf_state[...] = jnp.zeros_like(sf_state)
        sb_state[...] = jnp.zeros_like(sb_state)

        def scan_body(t, cy):
            for u in range(2):
                scan_step(c0 + 2 * t + u, True)
                scan_step(c0 + n_chunks - 1 - 2 * t - u, False)
            return cy

        lax.fori_loop(0, n_chunks // 2, scan_body, 0)

        def out_body(t, cy):
            for u in range(SCAN_UNROLL):
                outputs(c0 + SCAN_UNROLL * t + u)
            return cy

        lax.fori_loop(0, n_chunks // SCAN_UNROLL, out_body, 0)
        return carry

    lax.fori_loop(0, n_sub, sub_body, 0)


def _ssd_call(ssd_in, dt_raw, lw, segs, blk):
    n_tok = ssd_in.shape[0]
    nc = blk // CHUNK
    in_specs = [pl.BlockSpec((blk, SSD_IN_W), lambda s: (s, 0)),
                pl.BlockSpec((nc, 2 * SSD_HEADS, CHUNK), lambda s: (s, 0, 0)),
                _const_spec((2 * SSD_HEADS, 1)), _const_spec((2 * SSD_HEADS, 1)),
                _const_spec((1, SSD_WIDTH)), _const_spec((1, SSD_WIDTH))]
    scratch = [pltpu.VMEM((nc, 2 * SSD_HEADS, CHUNK), F32),
               pltpu.VMEM((nc, CHUNK, 128), F32),
               pltpu.VMEM((nc, SSD_STATE, SSD_WIDTH), BF16),
               pltpu.VMEM((nc, SSD_STATE, SSD_WIDTH), BF16),
               pltpu.VMEM((SSD_STATE, SSD_WIDTH), F32),
               pltpu.VMEM((SSD_STATE, SSD_WIDTH), F32),
               pltpu.VMEM((CHUNK, 4 * SSD_WIDTH), BF16)]
    return _mixer_call(functools.partial(_ssd_kernel, segs, blk), "ssd",
                       [ssd_in, dt_raw, lw["dt_bias"], lw["a_neg"], lw["d_skip"], lw["ssd_norm"]],
                       in_specs, SSD_WIDTH, n_tok, blk, scratch)


def _ret_kernel(segs, blk, in_ref, thq_ref, thv_ref, out_ref, sf_in_ref, sb_in_ref, sf_state, sb_state):
    q = CHUNK
    nh = RET_HEADS
    seq_len, n_sub = _sub_sequences(segs, blk)
    n_chunks = seq_len // q
    lgq = -jnp.exp(thq_ref[...])
    lgv = -jnp.exp(thv_ref[...])
    ii = lax.broadcasted_iota(jnp.int32, (q, q), 0)
    jj = lax.broadcasted_iota(jnp.int32, (q, q), 1)
    causal = jj <= ii
    dist = (ii - jj).astype(F32)
    tok = lax.broadcasted_iota(jnp.int32, (q, 1), 0).astype(F32)

    w_f = jnp.exp((q - 1.0 - tok) * lgq[0:1, :])
    w_b = jnp.exp(tok * lgq[1:2, :])
    e_f = jnp.exp((tok + 1.0) * lgv[0:1, :])
    e_b = jnp.exp((q - tok) * lgv[1:2, :])
    decays = []
    dec_f, dec_b = [], []
    for h in range(nh):
        lf = lgv[0:1, h * RET_V:(h + 1) * RET_V]
        lb = lgv[1:2, h * RET_V:(h + 1) * RET_V]
        decays.append(jnp.exp(jnp.where(causal, dist * lf, -dist * lb)))
        dec_f.append(jnp.broadcast_to(jnp.exp(float(q) * lf), (RET_QK, RET_V)))
        dec_b.append(jnp.broadcast_to(jnp.exp(float(q) * lb), (RET_QK, RET_V)))
    dec_f = jnp.concatenate(dec_f, axis=0)
    dec_b = jnp.concatenate(dec_b, axis=0)
    lane_q = lax.broadcasted_iota(jnp.int32, (q, RET_QK_W), 1) >> 6

    def scan_step(c, fwd):
        state_ref, rec_ref, wrow, dec = (sf_state, sf_in_ref, w_f, dec_f) if fwd else (sb_state, sb_in_ref, w_b, dec_b)
        rows = in_ref[pl.ds(pl.multiple_of(c * q, q), q), :]
        k = rows[:, RET_QK_W:2 * RET_QK_W]
        v = rows[:, 2 * RET_QK_W:2 * RET_QK_W + RET_WIDTH]
        rec_ref[c] = state_ref[...].astype(BF16)
        kw = (k.astype(F32) * wrow).T.astype(BF16)
        full = _dot(kw, v)
        ds = jnp.concatenate([full[h * RET_QK:(h + 1) * RET_QK, h * RET_V:(h + 1) * RET_V] for h in range(nh)],
                             axis=0)
        state_ref[...] = state_ref[...] * dec + ds

    def outputs(c):
        r0 = pl.multiple_of(c * q, q)
        rows = in_ref[pl.ds(r0, q), :]
        qv = rows[:, 0:RET_QK_W]
        kv = rows[:, RET_QK_W:2 * RET_QK_W]
        vv = rows[:, 2 * RET_QK_W:2 * RET_QK_W + RET_WIDTH]
        gv = rows[:, 2 * RET_QK_W + RET_WIDTH:RET_IN_W].astype(F32)
        zero = jnp.zeros_like(qv)
        k_bd = jnp.concatenate([jnp.where(lane_q == h, kv, zero) for h in range(nh)], axis=0)
        s_all = _dot_nt(qv, k_bd)
        sf_in = sf_in_ref[c]
        sb_in = sb_in_ref[c]
        outs = []
        for h in range(nh):
            sl = slice(h * RET_V, (h + 1) * RET_V)
            p = (s_all[:, h * q:(h + 1) * q] * decays[h]).astype(BF16)
            qh = jnp.where(lane_q == h, qv, zero)
            y = (_dot(p, vv[:, sl]) + _dot(qh, sf_in) * e_f[:, sl] + _dot(qh, sb_in) * e_b[:, sl])
            outs.append(y * lax.rsqrt(jnp.mean(y * y, axis=-1, keepdims=True) + EPS))
        o = jnp.concatenate(outs, axis=1) * _silu(gv)
        out_ref[pl.ds(r0, q), :] = o.astype(BF16)

    def sub_body(sub, carry):
        c0 = sub * n_chunks
        sf_state[...] = jnp.zeros_like(sf_state)
        sb_state[...] = jnp.zeros_like(sb_state)

        def scan_body(t, cy):
            scan_step(c0 + t, True)
            scan_step(c0 + n_chunks - 1 - t, False)
            return cy

        lax.fori_loop(0, n_chunks, scan_body, 0)

        def out_body(t, cy):
            for u in range(SCAN_UNROLL):
                outputs(c0 + SCAN_UNROLL * t + u)
            return cy

        lax.fori_loop(0, n_chunks // SCAN_UNROLL, out_body, 0)
        return carry

    lax.fori_loop(0, n_sub, sub_body, 0)


def _ret_call(ret_in, lw, segs, blk):
    n_tok = ret_in.shape[0]
    nc = blk // CHUNK
    in_specs = [pl.BlockSpec((blk, RET_IN_W), lambda s: (s, 0)),
                _const_spec((2, RET_QK_W)), _const_spec((2, RET_WIDTH))]
    scratch = [pltpu.VMEM((nc, RET_QK_W, RET_V), BF16),
               pltpu.VMEM((nc, RET_QK_W, RET_V), BF16),
               pltpu.VMEM((RET_QK_W, RET_V), F32),
               pltpu.VMEM((RET_QK_W, RET_V), F32)]
    return _mixer_call(functools.partial(_ret_kernel, segs, blk), "ret",
                       [ret_in, lw["theta_q"], lw["theta_v"]], in_specs, RET_WIDTH, n_tok, blk, scratch)


def _na_kernel(segs, blk, in_ref, bias_ref, out_ref, s_ref):
    w = GRID_W
    n_pairs = NA_HEADS // 2
    seq_len, n_sub = _sub_sequences(segs, blk)
    n_rows = seq_len // w
    lane = lax.broadcasted_iota(jnp.int32, (w, 128), 1)
    low = lane < NA_HEADDIM
    scale = NA_HEADDIM ** -0.5

    def sub_body(sub, carry):
        row0 = sub * n_rows

        def window(r):
            rs = jnp.clip(r - NA_ROWS // 2, 0, n_rows - NA_ROWS)
            return r - rs, pl.multiple_of((row0 + r) * w, w), pl.multiple_of((row0 + rs) * w, w)

        def logits_stage(r, slot):
            d, q0, k0 = window(r)
            for p in range(n_pairs):
                qp = in_ref[pl.ds(q0, w), p * 128:(p + 1) * 128] * scale
                zero = jnp.zeros_like(qp)
                lhs = jnp.concatenate([jnp.where(low, qp, zero), jnp.where(low, zero, qp)], axis=0)
                kp = in_ref[pl.ds(k0, NA_WIN), NA_WIDTH + p * 128:NA_WIDTH + (p + 1) * 128]
                s_ref[slot, p] = _dot_nt(lhs, kp) + bias_ref[d, p]

        def value_stage(r, slot):
            _, q0, k0 = window(r)
            for p in range(n_pairs):
                s = s_ref[slot, p]
                m = jnp.max(s, axis=-1, keepdims=True)
                e = jnp.exp(s - m)
                l = jnp.sum(e, axis=-1, keepdims=True)
                vp = in_ref[pl.ds(k0, NA_WIN), 2 * NA_WIDTH + p * 128:2 * NA_WIDTH + (p + 1) * 128]
                o = _dot(e.astype(BF16), vp) / l
                out_ref[pl.ds(q0, w), p * 128:(p + 1) * 128] = jnp.where(low, o[0:w], o[w:2 * w]).astype(BF16)

        logits_stage(jnp.int32(0), 0)

        def row_body(t, cy):
            r = 2 * t
            logits_stage(r + 1, 1)
            value_stage(r, 0)
            logits_stage(r + 2, 0)
            value_stage(r + 1, 1)
            return cy

        lax.fori_loop(0, n_rows // 2 - 1, row_body, 0)
        logits_stage(n_rows - 1, 1)
        value_stage(n_rows - 2, 0)
        value_stage(n_rows - 1, 1)
        return carry

    lax.fori_loop(0, n_sub, sub_body, 0)


def _na_call(na_in, bias, segs, blk):
    n_tok = na_in.shape[0]
    in_specs = [pl.BlockSpec((blk, NA_IN_W), lambda s: (s, 0)),
                _const_spec((NA_ROWS, NA_HEADS // 2, 2 * GRID_W, NA_WIN))]
    scratch = [pltpu.VMEM((2, NA_HEADS // 2, 2 * GRID_W, NA_WIN), F32)]
    return _mixer_call(functools.partial(_na_kernel, segs, blk), "na", [na_in, bias],
                       in_specs, NA_WIDTH, n_tok, blk, scratch)


def _na_bias_table(rpb):
    cols = np.arange(GRID_W)
    col_start = np.clip(cols - NA_COLS // 2, 0, GRID_W - NA_COLS)
    kc = np.arange(GRID_W)
    valid = (kc[None, :] >= col_start[:, None]) & (kc[None, :] < col_start[:, None] + NA_COLS)
    col_off = np.clip(kc[None, :] - cols[:, None] + NA_COLS - 1, 0, 2 * NA_COLS - 2)
    toep = jnp.where(valid[None, None], rpb.astype(F32)[:, :, col_off], NEG_BIG)
    per_shift = []
    for d in range(NA_ROWS):
        rows = [toep[:, x - d + NA_ROWS - 1] for x in range(NA_ROWS)]
        per_shift.append(jnp.stack(rows, axis=2).reshape(NA_HEADS, GRID_W, NA_WIN))
    table = jnp.stack(per_shift, axis=0)
    return table.reshape(NA_ROWS, NA_HEADS // 2, 2 * GRID_W, NA_WIN)


def _merge_kernel(x_ref, gates_ref, o0_ref, o1_ref, o2_ref, wb_ref, wo_ref, out_ref):
    merged = None
    for b, o_ref in enumerate((o0_ref, o1_ref, o2_ref)):
        t = gates_ref[:, b * D_MODEL:(b + 1) * D_MODEL].astype(F32) * _dot(o_ref[...], wb_ref[b])
        merged = t if merged is None else merged + t
    out_ref[...] = x_ref[...] + _dot(merged.astype(BF16), wo_ref[...])


def _merge_call(x, gates, o_ssd, o_ret, o_na, lw, tm):
    n_tok = x.shape[0]
    tok = lambda w: pl.BlockSpec((tm, w), lambda i: (i, 0))
    return pl.pallas_call(
        _merge_kernel,
        grid=(n_tok // tm,),
        in_specs=[tok(D_MODEL), tok(GATE_W), tok(SSD_WIDTH), tok(RET_WIDTH), tok(NA_WIDTH),
                  _const_spec((N_BRANCH, SSD_WIDTH, D_MODEL)), _const_spec((D_MODEL, D_MODEL))],
        out_specs=tok(D_MODEL),
        out_shape=jax.ShapeDtypeStruct((n_tok, D_MODEL), F32),
        compiler_params=_params(("arbitrary",)),
        name="merge",
    )(x, gates, o_ssd, o_ret, o_na, lw["w_branch"], lw["w_out"])


def _ffn_kernel(segs, tm, final, x_ref, xp_ref, xn_ref, g_ref, wup_ref, cw_ref, cb_ref, wdn_ref, gf_ref,
                out_ref, h_ref, acc_ref, u_ref):
    i = pl.program_id(0)
    pos0, seq_len = _tile_pos(i, tm, segs)
    g = g_ref[...]

    def rms(v):
        return v * lax.rsqrt(jnp.mean(v * v, axis=-1, keepdims=True) + EPS) * g

    x = x_ref[...]
    prev_ok = (pos0 > 0).astype(F32)
    next_ok = (pos0 + tm < seq_len).astype(F32)
    h_ref[0:HALO, :] = (rms(xp_ref[...]) * prev_ok).astype(BF16)
    h_ref[HALO:HALO + tm, :] = rms(x).astype(BF16)
    h_ref[HALO + tm:2 * HALO + tm, :] = (rms(xn_ref[...]) * next_ok).astype(BF16)
    acc_ref[...] = jnp.zeros_like(acc_ref)

    n_slabs = FF_CHUNK // 128

    def up_stage(c, slot):
        hcat = h_ref[...]
        for j, cc in enumerate((c, N_FF_CHUNKS + c)):
            u = _dot(hcat, wup_ref[cc])
            for s in range(n_slabs):
                u_ref[slot, j, s] = u[:, s * 128:(s + 1) * 128]

    def conv(slot, j, cw, cb):
        outs = []
        for s in range(n_slabs):
            sl = slice(s * 128, (s + 1) * 128)
            acc = cb[:, sl] + cw[0:1, sl] * u_ref[slot, j, s, HALO - 1:HALO - 1 + tm, :]
            for k in range(1, FFN_CONV):
                acc = acc + cw[k:k + 1, sl] * u_ref[slot, j, s, HALO - 1 + k:HALO - 1 + k + tm, :]
            outs.append(acc)
        return jnp.concatenate(outs, axis=1)

    def down_stage(c, slot):
        val = conv(slot, 0, cw_ref[c], cb_ref[c])
        gate = conv(slot, 1, cw_ref[N_FF_CHUNKS + c], cb_ref[N_FF_CHUNKS + c])
        act = (_silu(gate) * val).astype(BF16)
        acc_ref[...] += _dot(act, wdn_ref[c])

    assert N_FF_CHUNKS % 2 == 1
    up_stage(0, 0)

    def body(t, carry):
        c = 2 * t
        up_stage(c + 1, 1)
        down_stage(c, 0)
        up_stage(c + 2, 0)
        down_stage(c + 1, 1)
        return carry

    lax.fori_loop(0, N_FF_CHUNKS // 2, body, 0)
    down_stage(N_FF_CHUNKS - 1, 0)
    y = x + acc_ref[...]
    if final:
        y = y * lax.rsqrt(jnp.mean(y * y, axis=-1, keepdims=True) + EPS) * gf_ref[...]
    out_ref[...] = y


def _ffn_call(x, lw, norm_final, segs, tm, final):
    n_tok = x.shape[0]
    prev_spec, next_spec = _halo_specs(tm, n_tok)
    tok = pl.BlockSpec((tm, D_MODEL), lambda i: (i, 0))
    return pl.pallas_call(
        functools.partial(_ffn_kernel, segs, tm, final),
        grid=(n_tok // tm,),
        in_specs=[tok, prev_spec, next_spec, _const_spec((1, D_MODEL)),
                  _const_spec((2 * N_FF_CHUNKS, D_MODEL, FF_CHUNK)),
                  _const_spec((2 * N_FF_CHUNKS, FFN_CONV, FF_CHUNK)),
                  _const_spec((2 * N_FF_CHUNKS, 1, FF_CHUNK)),
                  _const_spec((N_FF_CHUNKS, FF_CHUNK, D_MODEL)),
                  _const_spec((1, D_MODEL))],
        out_specs=tok,
        out_shape=jax.ShapeDtypeStruct((n_tok, D_MODEL), F32),
        scratch_shapes=[pltpu.VMEM((tm + 2 * HALO, D_MODEL), BF16), pltpu.VMEM((tm, D_MODEL), F32),
                        pltpu.VMEM((2, 2, FF_CHUNK // 128, tm + 2 * HALO, 128), F32)],
        compiler_params=_params(("arbitrary",)),
        name="ffn",
    )(x, x, x, lw["norm_ffn"], lw["w_up"], lw["ffn_conv_w"], lw["ffn_conv_b"], lw["w_down"], norm_final)


def _layer_weights(i, norm_mix, w_in, gate_bias, ssd_conv_w, ssd_conv_b, ssd_dt_bias, ssd_a_log, ssd_d, ssd_norm,
                   ret_theta, w_branch, w_out, norm_ffn, ffn_w_up, ffn_conv_w, ffn_conv_b, ffn_w_down):
    w = w_in[i]
    offs = np.cumsum([0, SSD_WIDTH, SSD_XBC, 2 * SSD_HEADS, RET_IN_W, NA_IN_W, GATE_W])
    col = lambda k: w[:, offs[k]:offs[k + 1]]
    up = ffn_w_up[i].astype(BF16).reshape(D_MODEL, 2 * N_FF_CHUNKS, FF_CHUNK).transpose(1, 0, 2)
    return {
        "norm_mix": norm_mix[i].reshape(1, D_MODEL),
        "w_z": col(0).astype(BF16),
        "w_xbc": col(1).astype(BF16),
        "w_dt": col(2).T.astype(BF16),
        "w_ret": col(3).astype(BF16),
        "w_na": col(4).astype(BF16),
        "w_gate": col(5).astype(BF16),
        "gate_bias": gate_bias[i].reshape(1, GATE_W),
        "conv_w": ssd_conv_w[i],
        "conv_b": ssd_conv_b[i].reshape(1, SSD_XBC),
        "dt_bias": ssd_dt_bias[i].reshape(2 * SSD_HEADS, 1),
        "a_neg": -jnp.exp(ssd_a_log[i].astype(F32)).reshape(2 * SSD_HEADS, 1),
        "d_skip": jnp.repeat(ssd_d[i], SSD_HEADDIM).reshape(1, SSD_WIDTH),
        "ssd_norm": ssd_norm[i].reshape(1, SSD_WIDTH),
        "theta_q": jnp.repeat(ret_theta[i], RET_QK, axis=1),
        "theta_v": jnp.repeat(ret_theta[i], RET_V, axis=1),
        "w_branch": w_branch[i].astype(BF16),
        "w_out": w_out[i].astype(BF16),
        "norm_ffn": norm_ffn[i].reshape(1, D_MODEL),
        "w_up": up,
        "ffn_conv_w": ffn_conv_w[i].reshape(FFN_CONV, 2 * N_FF_CHUNKS, FF_CHUNK).transpose(1, 0, 2),
        "ffn_conv_b": ffn_conv_b[i].reshape(2 * N_FF_CHUNKS, 1, FF_CHUNK),
        "w_down": ffn_w_down[i].astype(BF16).reshape(N_FF_CHUNKS, FF_CHUNK, D_MODEL),
    }


def _rope_table(max_len):
    half = RET_QK // 2
    inv = 1.0 / (ROPE_BASE ** (jnp.arange(half, dtype=F32) / half))
    ang = jnp.arange(max_len, dtype=F32)[:, None] * inv[None, :]
    cos = jnp.tile(jnp.cos(ang), (1, 128 // half))
    sin = jnp.tile(jnp.concatenate([-jnp.sin(ang), jnp.sin(ang)], axis=1), (1, 128 // RET_QK))
    return jnp.concatenate([cos, sin], axis=1)


def _pick_tile(segs):
    tm = 512
    while any(seq_len % tm for _, seq_len in segs):
        tm //= 2
    assert tm >= CHUNK
    return tm


def _trunk(xs, norm_mix, w_in, gate_bias, ssd_conv_w, ssd_conv_b, ssd_dt_bias, ssd_a_log, ssd_d, ssd_norm,
           ret_theta, na_rpb, w_branch, w_out, norm_ffn, ffn_w_up, ffn_conv_w, ffn_conv_b, ffn_w_down, norm_final):
    segs = tuple((int(x.shape[0]), int(x.shape[1])) for x in xs)
    tm = _pick_tile(segs)
    blk = _mixer_block(segs)
    x = jnp.concatenate([x.reshape(-1, D_MODEL) for x in xs], axis=0)
    rope = _rope_table(max(seq_len for _, seq_len in segs))
    depth = w_in.shape[0]
    gf = norm_final.reshape(1, D_MODEL)
    for i in range(depth):
        lw = _layer_weights(i, norm_mix, w_in, gate_bias, ssd_conv_w, ssd_conv_b, ssd_dt_bias, ssd_a_log, ssd_d,
                            ssd_norm, ret_theta, w_branch, w_out, norm_ffn, ffn_w_up, ffn_conv_w, ffn_conv_b,
                            ffn_w_down)
        gates, ssd_in, dt_raw, ret_in, na_in = _proj_call(x, lw, rope, segs, tm)
        o_ssd = _ssd_call(ssd_in, dt_raw, lw, segs, blk)
        o_ret = _ret_call(ret_in, lw, segs, blk)
        o_na = _na_call(na_in, _na_bias_table(na_rpb[i]), segs, blk)
        x = _merge_call(x, gates, o_ssd, o_ret, o_na, lw, tm)
        x = _ffn_call(x, lw, gf, segs, tm, final=(i == depth - 1))
    outs = []
    base = 0
    for (n_seq, seq_len), x_in in zip(segs, xs):
        outs.append(x[base:base + n_seq * seq_len].reshape(x_in.shape))
        base += n_seq * seq_len
    return tuple(outs)


def kernel(x_prompt, x_sample, norm_mix, w_in, gate_bias, ssd_conv_w, ssd_conv_b, ssd_dt_bias, ssd_a_log, ssd_d,
           ssd_norm, ret_theta, na_rpb, w_branch, w_out, norm_ffn, ffn_w_up, ffn_conv_w, ffn_conv_b, ffn_w_down,
           norm_final):
    return _trunk((x_prompt, x_sample), norm_mix, w_in, gate_bias, ssd_conv_w, ssd_conv_b, ssd_dt_bias, ssd_a_log,
                  ssd_d, ssd_norm, ret_theta, na_rpb, w_branch, w_out, norm_ffn, ffn_w_up, ffn_conv_w, ffn_conv_b,
                  ffn_w_down, norm_final)
```

```python
import functools

import jax
import jax.numpy as jnp
import numpy as np
from jax import lax
from jax.experimental import pallas as pl
from jax.experimental.pallas import tpu as pltpu

F32 = jnp.float32
BF16 = jnp.bfloat16

D_MODEL = 1024
GRID_W = 64
CHUNK = 128
SSD_HEADS = 8
SSD_HEADDIM = 64
SSD_WIDTH = SSD_HEADS * SSD_HEADDIM
SSD_GROUPS = 2
SSD_STATE = 128
SSD_BC = SSD_GROUPS * SSD_STATE
SSD_XBC = SSD_WIDTH + 2 * SSD_BC
SSD_CONV = 5
RET_HEADS = 4
RET_QK = 64
RET_V = 128
RET_WIDTH = RET_HEADS * RET_V
ROPE_BASE = 10000.0
NA_HEADS = 8
NA_HEADDIM = 64
NA_WIDTH = NA_HEADS * NA_HEADDIM
NA_ROWS = 8
NA_COLS = 16
N_BRANCH = 3
D_FF = 2816
FFN_CONV = 3
EPS = 1e-6

GATE_W = N_BRANCH * D_MODEL
SSD_IN_W = SSD_WIDTH + SSD_XBC
RET_QK_W = RET_HEADS * RET_QK
RET_IN_W = 2 * RET_QK_W + 2 * RET_WIDTH
NA_IN_W = 3 * NA_WIDTH
NA_WIN = NA_ROWS * GRID_W
HALO = 16
FF_CHUNK = 256
N_FF_CHUNKS = D_FF // FF_CHUNK
NEG_BIG = -1e30
VMEM_LIMIT = 56 * 1024 * 1024
PROJ_TILE = 512
FFN_TILE = 1024
SCAN_UNROLL = 4


def _sigmoid(v):
    return 1.0 / (1.0 + jnp.exp(-v))


def _silu(v):
    return v * _sigmoid(v)


def _dot(a, b):
    return jnp.dot(a, b, preferred_element_type=F32)


def _dot_nt(a, b):
    return lax.dot_general(a, b, (((1,), (1,)), ((), ())), preferred_element_type=F32)


def _dot3(x, m01):
    x1 = x.astype(BF16)
    r1 = x - x1.astype(F32)
    x2 = r1.astype(BF16)
    x3 = (r1 - x2.astype(F32)).astype(BF16)
    return _dot(x1, m01) + _dot(x2, m01) + _dot(x3, m01)


def _tile_pos(i, tm, segs):
    t0 = i * tm
    pos = None
    length = None
    base = 0
    for n_seq, seq_len in segs:
        p = lax.rem(t0 - base, seq_len)
        if pos is None:
            pos, length = p, jnp.int32(seq_len)
        else:
            pos = jnp.where(t0 >= base, p, pos)
            length = jnp.where(t0 >= base, jnp.int32(seq_len), length)
        base += n_seq * seq_len
    return pos, length


def _const_spec(shape):
    nd = len(shape)
    return pl.BlockSpec(shape, lambda *_: (0,) * nd, pipeline_mode=pl.Buffered(1))


def _params(sem):
    return pltpu.CompilerParams(dimension_semantics=sem, vmem_limit_bytes=VMEM_LIMIT)


def _proj_kernel(segs, tm, x_ref, xp_ref, xn_ref, g_ref, wg_ref, gb_ref, wz_ref, wx_ref, cw_ref, cb_ref,
                 wdt_ref, wr_ref, rope_ref, wn_ref, gates_ref, ssd_ref, dt_ref, ret_ref, na_ref, u_ref):
    i = pl.program_id(0)
    pos0, seq_len = _tile_pos(i, tm, segs)
    g = g_ref[...]

    def rms(v):
        return v * lax.rsqrt(jnp.mean(v * v, axis=-1, keepdims=True) + EPS) * g

    h = rms(x_ref[...]).astype(BF16)
    prev_ok = (pos0 > 0).astype(F32)
    next_ok = (pos0 + tm < seq_len).astype(F32)
    hp = (rms(xp_ref[...]) * prev_ok).astype(BF16)
    hn = (rms(xn_ref[...]) * next_ok).astype(BF16)
    hcat = jnp.concatenate([hp, h, hn], axis=0)

    for c in range(GATE_W // 512):
        sl = slice(c * 512, (c + 1) * 512)
        a = _dot(h, wg_ref[:, sl]) + gb_ref[:, sl]
        gates_ref[:, sl] = _sigmoid(a).astype(BF16)

    ssd_ref[:, 0:SSD_WIDTH] = _dot(h, wz_ref[...]).astype(BF16)
    for c in range(SSD_XBC // 512):
        u = _dot(hcat, wx_ref[:, c * 512:(c + 1) * 512])
        for s in range(4):
            u_ref[4 * c + s] = u[:, s * 128:(s + 1) * 128]
    for s in range(SSD_XBC // 128):
        sl = slice(s * 128, (s + 1) * 128)
        acc = cb_ref[:, sl] + cw_ref[0:1, sl] * u_ref[s, HALO - 2:HALO - 2 + tm, :]
        for k in range(1, SSD_CONV):
            acc = acc + cw_ref[k:k + 1, sl] * u_ref[s, HALO - 2 + k:HALO - 2 + k + tm, :]
        ssd_ref[:, SSD_WIDTH + s * 128:SSD_WIDTH + (s + 1) * 128] = _silu(acc).astype(BF16)

    dt_t = _dot_nt(wdt_ref[...], h)
    for k in range(tm // CHUNK):
        dt_ref[k] = dt_t[:, k * CHUNK:(k + 1) * CHUNK]

    cos = rope_ref[:, 0:128]
    sin = rope_ref[:, 128:256]
    lane = lax.broadcasted_iota(jnp.int32, (tm, 128), 1)
    first_half = (lane & (RET_QK - 1)) < RET_QK // 2

    def rotary(t, scale):
        outs = []
        for j in range(RET_QK_W // 128):
            tj = t[:, j * 128:(j + 1) * 128]
            partner = jnp.where(first_half, pltpu.roll(tj, 128 - RET_QK // 2, axis=1),
                                pltpu.roll(tj, RET_QK // 2, axis=1))
            outs.append((tj * cos + partner * sin) * scale)
        return jnp.concatenate(outs, axis=1)

    ret_ref[:, 0:RET_QK_W] = rotary(_dot(h, wr_ref[:, 0:RET_QK_W]), 1.0).astype(BF16)
    ret_ref[:, RET_QK_W:2 * RET_QK_W] = rotary(_dot(h, wr_ref[:, RET_QK_W:2 * RET_QK_W]),
                                              RET_QK ** -0.5).astype(BF16)
    for c in range(2):
        sl = slice(2 * RET_QK_W + c * 512, 2 * RET_QK_W + (c + 1) * 512)
        ret_ref[:, sl] = _dot(h, wr_ref[:, sl]).astype(BF16)

    for c in range(NA_IN_W // 512):
        sl = slice(c * 512, (c + 1) * 512)
        na_ref[:, sl] = _dot(h, wn_ref[:, sl]).astype(BF16)


def _halo_specs(tm, n_tok):
    r = tm // HALO
    last = n_tok // HALO - 1
    prev = pl.BlockSpec((HALO, D_MODEL), lambda i: (jnp.maximum(i * r - 1, 0), 0))
    nxt = pl.BlockSpec((HALO, D_MODEL), lambda i: (jnp.minimum((i + 1) * r, last), 0))
    return prev, nxt


def _proj_call(x, lw, rope, segs, tm):
    n_tok = x.shape[0]

    def rope_map(i):
        pos0, _ = _tile_pos(i, tm, segs)
        return (pos0 // tm, 0)

    prev_spec, next_spec = _halo_specs(tm, n_tok)
    tok = lambda w: pl.BlockSpec((tm, w), lambda i: (i, 0))
    return pl.pallas_call(
        functools.partial(_proj_kernel, segs, tm),
        grid=(n_tok // tm,),
        in_specs=[tok(D_MODEL), prev_spec, next_spec,
                  _const_spec((1, D_MODEL)),
                  _const_spec((D_MODEL, GATE_W)), _const_spec((1, GATE_W)),
                  _const_spec((D_MODEL, SSD_WIDTH)),
                  _const_spec((D_MODEL, SSD_XBC)), _const_spec((SSD_CONV, SSD_XBC)), _const_spec((1, SSD_XBC)),
                  _const_spec((2 * SSD_HEADS, D_MODEL)),
                  _const_spec((D_MODEL, RET_IN_W)),
                  pl.BlockSpec((tm, 256), rope_map),
                  _const_spec((D_MODEL, NA_IN_W))],
        out_specs=[tok(GATE_W), tok(SSD_IN_W),
                   pl.BlockSpec((tm // CHUNK, 2 * SSD_HEADS, CHUNK), lambda i: (i, 0, 0)),
                   tok(RET_IN_W), tok(NA_IN_W)],
        out_shape=[jax.ShapeDtypeStruct((n_tok, GATE_W), BF16),
                   jax.ShapeDtypeStruct((n_tok, SSD_IN_W), BF16),
                   jax.ShapeDtypeStruct((n_tok // CHUNK, 2 * SSD_HEADS, CHUNK), F32),
                   jax.ShapeDtypeStruct((n_tok, RET_IN_W), BF16),
                   jax.ShapeDtypeStruct((n_tok, NA_IN_W), BF16)],
        scratch_shapes=[pltpu.VMEM((SSD_XBC // 128, tm + 2 * HALO, 128), F32)],
        compiler_params=_params(("arbitrary",)),
        name="proj",
    )(x, x, x, lw["norm_mix"], lw["w_gate"], lw["gate_bias"], lw["w_z"], lw["w_xbc"], lw["conv_w"],
      lw["conv_b"], lw["w_dt"], lw["w_ret"], rope, lw["w_na"])


def _mixer_block(segs):
    blk = max(seq_len for _, seq_len in segs)
    base = 0
    for n_seq, seq_len in segs:
        assert blk % seq_len == 0 and base % blk == 0 and (n_seq * seq_len) % blk == 0
        assert seq_len % (SCAN_UNROLL * CHUNK) == 0 and seq_len // GRID_W >= NA_ROWS
        base += n_seq * seq_len
    return blk


def _sub_sequences(segs, blk):
    _, seq_len = _tile_pos(pl.program_id(0), blk, segs)
    return seq_len, blk // seq_len


def _mixer_call(kernel_fn, name, inputs, in_specs, out_width, n_tok, blk, scratch):
    return pl.pallas_call(
        kernel_fn,
        grid=(n_tok // blk,),
        in_specs=in_specs,
        out_specs=pl.BlockSpec((blk, out_width), lambda s: (s, 0)),
        out_shape=jax.ShapeDtypeStruct((n_tok, out_width), BF16),
        scratch_shapes=scratch,
        compiler_params=_params(("arbitrary",)),
        name=name,
    )(*inputs)


def _ssd_kernel(segs, blk, in_ref, dt_ref, dtb_ref, a_ref, dsk_ref, ng_ref, out_ref,
                cadj_ref, pt_ref, sf_in_ref, sb_in_ref, sf_state, sb_state, exp_ref):
    q = CHUNK
    nh = SSD_HEADS
    gw = SSD_WIDTH // SSD_GROUPS
    seq_len, n_sub = _sub_sequences(segs, blk)
    n_chunks = seq_len // q
    ii = lax.broadcasted_iota(jnp.int32, (q, q), 0)
    jj = lax.broadcasted_iota(jnp.int32, (q, q), 1)
    causal = jj <= ii
    pre_m = (ii <= jj).astype(BF16)
    suf_m = (ii >= jj).astype(BF16)
    row16 = lax.broadcasted_iota(jnp.int32, (2 * nh, q), 0)
    is_fwd_row = row16 < nh
    lane_g = lax.broadcasted_iota(jnp.int32, (q, gw), 1)

    er = lax.broadcasted_iota(jnp.int32, (q, 4 * SSD_WIDTH), 0)
    ec = lax.broadcasted_iota(jnp.int32, (q, 4 * SSD_WIDTH), 1)
    ecc = ec & (2 * SSD_WIDTH - 1)
    hd = ((ecc >> 9) << 3) + ((ecc & (SSD_WIDTH - 1)) >> 6)
    exp_ref[...] = (er == jnp.where(ec < 2 * SSD_WIDTH, 2 * nh, 4 * nh) + hd).astype(BF16)

    dtb = dtb_ref[...]
    a_neg = a_ref[...]

    def prep(c):
        v = dt_ref[c] + dtb
        dtv = jnp.maximum(v, 0.0) + jnp.log1p(jnp.exp(-jnp.abs(v)))
        la = dtv * a_neg
        pre = _dot3(la, pre_m)
        suf = _dot3(la, suf_m)
        cum = jnp.where(is_fwd_row, pre, suf)
        tot = jnp.where(is_fwd_row[:, 0:1], pre[:, q - 1:q], suf[:, 0:1])
        wgt = jnp.exp(tot - cum) * dtv
        esc = jnp.exp(cum)
        pack = jnp.concatenate([cum, wgt, esc, jnp.zeros((q - 6 * nh, q), F32)], axis=0)
        cadj_ref[c] = cum - jnp.log(dtv)
        pt_ref[c] = pack.T

    def scan_step(c, fwd):
        state_ref, rec_ref = (sf_state, sf_in_ref) if fwd else (sb_state, sb_in_ref)
        w_cols = slice(0, SSD_WIDTH) if fwd else slice(SSD_WIDTH, 2 * SSD_WIDTH)
        e_cols = slice(2 * SSD_WIDTH, 3 * SSD_WIDTH) if fwd else slice(3 * SSD_WIDTH, 4 * SSD_WIDTH)
        rows = in_ref[pl.ds(pl.multiple_of(c * q, q), q), :]
        ptb = pt_ref[c].astype(BF16)
        w = _dot(ptb, exp_ref[:, w_cols])
        edge = _dot(ptb[q - HALO:q] if fwd else ptb[0:HALO], exp_ref[:, e_cols])
        decay_row = edge[HALO - 1:HALO] if fwd else edge[0:1]
        rec_ref[c] = state_ref[...].astype(BF16)
        xw = (rows[:, SSD_WIDTH:2 * SSD_WIDTH].astype(F32) * w).astype(BF16)
        ds = []
        for gidx in range(SSD_GROUPS):
            b_off = 2 * SSD_WIDTH + gidx * SSD_STATE
            b_t = rows[:, b_off:b_off + SSD_STATE].astype(F32).T.astype(BF16)
            ds.append(_dot(b_t, xw[:, gidx * gw:(gidx + 1) * gw]))
        state_ref[...] = state_ref[...] * decay_row + jnp.concatenate(ds, axis=1)

    def outputs(c):
        r0 = pl.multiple_of(c * q, q)
        rows = in_ref[pl.ds(r0, q), :]
        z = rows[:, 0:SSD_WIDTH].astype(F32)
        xs_b = rows[:, SSD_WIDTH:2 * SSD_WIDTH]
        xs = xs_b.astype(F32)
        cadj = cadj_ref[c]
        pt = pt_ref[c]
        e_fb = _dot(pt.astype(BF16), exp_ref[:, 2 * SSD_WIDTH:4 * SSD_WIDTH])
        e_f = e_fb[:, 0:SSD_WIDTH]
        e_b = e_fb[:, SSD_WIDTH:2 * SSD_WIDTH]
        sf_in = sf_in_ref[c]
        sb_in = sb_in_ref[c]
        ys = []
        for gidx in range(SSD_GROUPS):
            c_off = 2 * SSD_WIDTH + SSD_BC + gidx * SSD_STATE
            b_off = 2 * SSD_WIDTH + gidx * SSD_STATE
            cg = rows[:, c_off:c_off + SSD_STATE]
            bg = rows[:, b_off:b_off + SSD_STATE]
            scores = _dot_nt(cg, bg)
            xg = xs_b[:, gidx * gw:(gidx + 1) * gw]
            yg = jnp.zeros((q, gw), F32)
            for r in range(SSD_HEADS // SSD_GROUPS):
                hf = gidx * (SSD_HEADS // SSD_GROUPS) + r
                hb = nh + hf
                seg_f = jnp.broadcast_to(pt[:, hf:hf + 1], (q, q)) - jnp.broadcast_to(cadj[hf:hf + 1, :], (q, q))
                seg_b = jnp.broadcast_to(pt[:, hb:hb + 1], (q, q)) - jnp.broadcast_to(cadj[hb:hb + 1, :], (q, q))
                m = (scores * jnp.exp(jnp.where(causal, seg_f, seg_b))).astype(BF16)
                yr = _dot(m, xg)
                yg = jnp.where((lane_g >> 6) == r, yr, yg)
            sl = slice(gidx * gw, (gidx + 1) * gw)
            y_inter = _dot(cg, sf_in[:, sl]) * e_f[:, sl] + _dot(cg, sb_in[:, sl]) * e_b[:, sl]
            ys.append(yg + y_inter)
        y = jnp.concatenate(ys, axis=1) + dsk_ref[...] * xs
        o = y * _silu(z)
        o = o * lax.rsqrt(jnp.mean(o * o, axis=-1, keepdims=True) + EPS) * ng_ref[...]
        out_ref[pl.ds(r0, q), :] = o.astype(BF16)

    def sub_body(sub, carry):
        c0 = sub * n_chunks

        def prep_body(t, cy):
            for u in range(SCAN_UNROLL):
                prep(c0 + SCAN_UNROLL * t + u)
            return cy

        lax.fori_loop(0, n_chunks // SCAN_UNROLL, prep_body, 0)

        sf_state[...] = jnp.zeros_like(sf_state)
        sb_state[...] = jnp.zeros_like(sb_state)

        def scan_body(t, cy):
            for u in range(2):
                scan_step(c0 + 2 * t + u, True)
                scan_step(c0 + n_chunks - 1 - 2 * t - u, False)
            return cy

        lax.fori_loop(0, n_chunks // 2, scan_body, 0)

        def out_body(t, cy):
            for u in range(SCAN_UNROLL):
                outputs(c0 + SCAN_UNROLL * t + u)
            return cy

        lax.fori_loop(0, n_chunks // SCAN_UNROLL, out_body, 0)
        return carry

    lax.fori_loop(0, n_sub, sub_body, 0)


def _ssd_call(ssd_in, dt_raw, lw, segs, blk):
    n_tok = ssd_in.shape[0]
    nc = blk // CHUNK
    in_specs = [pl.BlockSpec((blk, SSD_IN_W), lambda s: (s, 0)),
                pl.BlockSpec((nc, 2 * SSD_HEADS, CHUNK), lambda s: (s, 0, 0)),
                _const_spec((2 * SSD_HEADS, 1)), _const_spec((2 * SSD_HEADS, 1)),
                _const_spec((1, SSD_WIDTH)), _const_spec((1, SSD_WIDTH))]
    scratch = [pltpu.VMEM((nc, 2 * SSD_HEADS, CHUNK), F32),
               pltpu.VMEM((nc, CHUNK, 128), F32),
               pltpu.VMEM((nc, SSD_STATE, SSD_WIDTH), BF16),
               pltpu.VMEM((nc, SSD_STATE, SSD_WIDTH), BF16),
               pltpu.VMEM((SSD_STATE, SSD_WIDTH), F32),
               pltpu.VMEM((SSD_STATE, SSD_WIDTH), F32),
               pltpu.VMEM((CHUNK, 4 * SSD_WIDTH), BF16)]
    return _mixer_call(functools.partial(_ssd_kernel, segs, blk), "ssd",
                       [ssd_in, dt_raw, lw["dt_bias"], lw["a_neg"], lw["d_skip"], lw["ssd_norm"]],
                       in_specs, SSD_WIDTH, n_tok, blk, scratch)


def _ret_kernel(segs, blk, in_ref, thq_ref, thv_ref, out_ref, sf_in_ref, sb_in_ref, sf_state, sb_state):
    q = CHUNK
    nh = RET_HEADS
    seq_len, n_sub = _sub_sequences(segs, blk)
    n_chunks = seq_len // q
    lgq = -jnp.exp(thq_ref[...])
    lgv = -jnp.exp(thv_ref[...])
    ii = lax.broadcasted_iota(jnp.int32, (q, q), 0)
    jj = lax.broadcasted_iota(jnp.int32, (q, q), 1)
    causal = jj <= ii
    dist = (ii - jj).astype(F32)
    tok = lax.broadcasted_iota(jnp.int32, (q, 1), 0).astype(F32)

    w_f = jnp.exp((q - 1.0 - tok) * lgq[0:1, :])
    w_b = jnp.exp(tok * lgq[1:2, :])
    e_f = jnp.exp((tok + 1.0) * lgv[0:1, :])
    e_b = jnp.exp((q - tok) * lgv[1:2, :])
    decays = []
    dec_f, dec_b = [], []
    for h in range(nh):
        lf = lgv[0:1, h * RET_V:(h + 1) * RET_V]
        lb = lgv[1:2, h * RET_V:(h + 1) * RET_V]
        decays.append(jnp.exp(jnp.where(causal, dist * lf, -dist * lb)))
        dec_f.append(jnp.broadcast_to(jnp.exp(float(q) * lf), (RET_QK, RET_V)))
        dec_b.append(jnp.broadcast_to(jnp.exp(float(q) * lb), (RET_QK, RET_V)))
    dec_f = jnp.concatenate(dec_f, axis=0)
    dec_b = jnp.concatenate(dec_b, axis=0)
    lane_q = lax.broadcasted_iota(jnp.int32, (q, RET_QK_W), 1) >> 6

    def scan_step(c, fwd):
        state_ref, rec_ref, wrow, dec = (sf_state, sf_in_ref, w_f, dec_f) if fwd else (sb_state, sb_in_ref, w_b, dec_b)
        rows = in_ref[pl.ds(pl.multiple_of(c * q, q), q), :]
        k = rows[:, RET_QK_W:2 * RET_QK_W]
        v = rows[:, 2 * RET_QK_W:2 * RET_QK_W + RET_WIDTH]
        rec_ref[c] = state_ref[...].astype(BF16)
        kw = (k.astype(F32) * wrow).T.astype(BF16)
        full = _dot(kw, v)
        ds = jnp.concatenate([full[h * RET_QK:(h + 1) * RET_QK, h * RET_V:(h + 1) * RET_V] for h in range(nh)],
                             axis=0)
        state_ref[...] = state_ref[...] * dec + ds

    def outputs(c):
        r0 = pl.multiple_of(c * q, q)
        rows = in_ref[pl.ds(r0, q), :]
        qv = rows[:, 0:RET_QK_W]
        kv = rows[:, RET_QK_W:2 * RET_QK_W]
        vv = rows[:, 2 * RET_QK_W:2 * RET_QK_W + RET_WIDTH]
        gv = rows[:, 2 * RET_QK_W + RET_WIDTH:RET_IN_W].astype(F32)
        zero = jnp.zeros_like(qv)
        k_bd = jnp.concatenate([jnp.where(lane_q == h, kv, zero) for h in range(nh)], axis=0)
        s_all = _dot_nt(qv, k_bd)
        sf_in = sf_in_ref[c]
        sb_in = sb_in_ref[c]
        outs = []
        for h in range(nh):
            sl = slice(h * RET_V, (h + 1) * RET_V)
            p = (s_all[:, h * q:(h + 1) * q] * decays[h]).astype(BF16)
            qh = jnp.where(lane_q == h, qv, zero)
            y = (_dot(p, vv[:, sl]) + _dot(qh, sf_in) * e_f[:, sl] + _dot(qh, sb_in) * e_b[:, sl])
            outs.append(y * lax.rsqrt(jnp.mean(y * y, axis=-1, keepdims=True) + EPS))
        o = jnp.concatenate(outs, axis=1) * _silu(gv)
        out_ref[pl.ds(r0, q), :] = o.astype(BF16)

    def sub_body(sub, carry):
        c0 = sub * n_chunks
        sf_state[...] = jnp.zeros_like(sf_state)
        sb_state[...] = jnp.zeros_like(sb_state)

        def scan_body(t, cy):
            scan_step(c0 + t, True)
            scan_step(c0 + n_chunks - 1 - t, False)
            return cy

        lax.fori_loop(0, n_chunks, scan_body, 0)

        def out_body(t, cy):
            for u in range(SCAN_UNROLL):
                outputs(c0 + SCAN_UNROLL * t + u)
            return cy

        lax.fori_loop(0, n_chunks // SCAN_UNROLL, out_body, 0)
        return carry

    lax.fori_loop(0, n_sub, sub_body, 0)


def _ret_call(ret_in, lw, segs, blk):
    n_tok = ret_in.shape[0]
    nc = blk // CHUNK
    in_specs = [pl.BlockSpec((blk, RET_IN_W), lambda s: (s, 0)),
                _const_spec((2, RET_QK_W)), _const_spec((2, RET_WIDTH))]
    scratch = [pltpu.VMEM((nc, RET_QK_W, RET_V), BF16),
               pltpu.VMEM((nc, RET_QK_W, RET_V), BF16),
               pltpu.VMEM((RET_QK_W, RET_V), F32),
               pltpu.VMEM((RET_QK_W, RET_V), F32)]
    return _mixer_call(functools.partial(_ret_kernel, segs, blk), "ret",
                       [ret_in, lw["theta_q"], lw["theta_v"]], in_specs, RET_WIDTH, n_tok, blk, scratch)


def _na_kernel(segs, blk, in_ref, bias_ref, out_ref, s_ref):
    w = GRID_W
    n_pairs = NA_HEADS // 2
    seq_len, n_sub = _sub_sequences(segs, blk)
    n_rows = seq_len // w
    lane = lax.broadcasted_iota(jnp.int32, (w, 128), 1)
    low = lane < NA_HEADDIM
    scale = NA_HEADDIM ** -0.5

    def sub_body(sub, carry):
        row0 = sub * n_rows

        def window(r):
            rs = jnp.clip(r - NA_ROWS // 2, 0, n_rows - NA_ROWS)
            return r - rs, pl.multiple_of((row0 + r) * w, w), pl.multiple_of((row0 + rs) * w, w)

        def logits_stage(r, slot):
            d, q0, k0 = window(r)
            for p in range(n_pairs):
                qp = in_ref[pl.ds(q0, w), p * 128:(p + 1) * 128] * scale
                zero = jnp.zeros_like(qp)
                lhs = jnp.concatenate([jnp.where(low, qp, zero), jnp.where(low, zero, qp)], axis=0)
                kp = in_ref[pl.ds(k0, NA_WIN), NA_WIDTH + p * 128:NA_WIDTH + (p + 1) * 128]
                s_ref[slot, p] = _dot_nt(lhs, kp) + bias_ref[d, p]

        def value_stage(r, slot):
            _, q0, k0 = window(r)
            for p in range(n_pairs):
                s = s_ref[slot, p]
                m = jnp.max(s, axis=-1, keepdims=True)
                e = jnp.exp(s - m)
                l = jnp.sum(e, axis=-1, keepdims=True)
                vp = in_ref[pl.ds(k0, NA_WIN), 2 * NA_WIDTH + p * 128:2 * NA_WIDTH + (p + 1) * 128]
                o = _dot(e.astype(BF16), vp) / l
                out_ref[pl.ds(q0, w), p * 128:(p + 1) * 128] = jnp.where(low, o[0:w], o[w:2 * w]).astype(BF16)

        logits_stage(jnp.int32(0), 0)

        def row_body(t, cy):
            r = 2 * t
            logits_stage(r + 1, 1)
            value_stage(r, 0)
            logits_stage(r + 2, 0)
            value_stage(r + 1, 1)
            return cy

        lax.fori_loop(0, n_rows // 2 - 1, row_body, 0)
        logits_stage(n_rows - 1, 1)
        value_stage(n_rows - 2, 0)
        value_stage(n_rows - 1, 1)
        return carry

    lax.fori_loop(0, n_sub, sub_body, 0)


def _na_call(na_in, bias, segs, blk):
    n_tok = na_in.shape[0]
    in_specs = [pl.BlockSpec((blk, NA_IN_W), lambda s: (s, 0)),
                _const_spec((NA_ROWS, NA_HEADS // 2, 2 * GRID_W, NA_WIN))]
    scratch = [pltpu.VMEM((2, NA_HEADS // 2, 2 * GRID_W, NA_WIN), F32)]
    return _mixer_call(functools.partial(_na_kernel, segs, blk), "na", [na_in, bias],
                       in_specs, NA_WIDTH, n_tok, blk, scratch)


def _na_bias_table(rpb):
    cols = np.arange(GRID_W)
    col_start = np.clip(cols - NA_COLS // 2, 0, GRID_W - NA_COLS)
    kc = np.arange(GRID_W)
    valid = (kc[None, :] >= col_start[:, None]) & (kc[None, :] < col_start[:, None] + NA_COLS)
    col_off = np.clip(kc[None, :] - cols[:, None] + NA_COLS - 1, 0, 2 * NA_COLS - 2)
    toep = jnp.where(valid[None, None], rpb.astype(F32)[:, :, col_off], NEG_BIG)
    per_shift = []
    for d in range(NA_ROWS):
        rows = [toep[:, x - d + NA_ROWS - 1] for x in range(NA_ROWS)]
        per_shift.append(jnp.stack(rows, axis=2).reshape(NA_HEADS, GRID_W, NA_WIN))
    table = jnp.stack(per_shift, axis=0)
    return table.reshape(NA_ROWS, NA_HEADS // 2, 2 * GRID_W, NA_WIN)


def _merge_kernel(x_ref, gates_ref, o0_ref, o1_ref, o2_ref, wb_ref, wo_ref, out_ref):
    merged = None
    for b, o_ref in enumerate((o0_ref, o1_ref, o2_ref)):
        t = gates_ref[:, b * D_MODEL:(b + 1) * D_MODEL].astype(F32) * _dot(o_ref[...], wb_ref[b])
        merged = t if merged is None else merged + t
    out_ref[...] = x_ref[...] + _dot(merged.astype(BF16), wo_ref[...])


def _merge_call(x, gates, o_ssd, o_ret, o_na, lw, tm):
    n_tok = x.shape[0]
    tok = lambda w: pl.BlockSpec((tm, w), lambda i: (i, 0))
    return pl.pallas_call(
        _merge_kernel,
        grid=(n_tok // tm,),
        in_specs=[tok(D_MODEL), tok(GATE_W), tok(SSD_WIDTH), tok(RET_WIDTH), tok(NA_WIDTH),
                  _const_spec((N_BRANCH, SSD_WIDTH, D_MODEL)), _const_spec((D_MODEL, D_MODEL))],
        out_specs=tok(D_MODEL),
        out_shape=jax.ShapeDtypeStruct((n_tok, D_MODEL), F32),
        compiler_params=_params(("arbitrary",)),
        name="merge",
    )(x, gates, o_ssd, o_ret, o_na, lw["w_branch"], lw["w_out"])


def _ffn_kernel(segs, tm, final, x_ref, xp_ref, xn_ref, g_ref, wup_ref, cw_ref, cb_ref, wdn_ref, gf_ref,
                out_ref, h_ref, acc_ref, u_ref):
    i = pl.program_id(0)
    pos0, seq_len = _tile_pos(i, tm, segs)
    g = g_ref[...]

    def rms(v):
        return v * lax.rsqrt(jnp.mean(v * v, axis=-1, keepdims=True) + EPS) * g

    prev_ok = (pos0 > 0).astype(F32)
    next_ok = (pos0 + tm < seq_len).astype(F32)
    n_slabs = FF_CHUNK // 128
    n_parts = 4
    part = tm // n_parts
    assert part % HALO == 0

    def up_rows(c, slot, r0, r1):
        hs = h_ref[r0:r1, :]
        for j, cc in enumerate((c, N_FF_CHUNKS + c)):
            u = _dot(hs, wup_ref[cc])
            for s in range(n_slabs):
                u_ref[slot, j, s, r0:r1, :] = u[:, s * 128:(s + 1) * 128]

    def up_stage(c, slot):
        up_rows(c, slot, 0, tm + 2 * HALO)

    def conv(slot, j, cw, cb, r0, n):
        outs = []
        for s in range(n_slabs):
            sl = slice(s * 128, (s + 1) * 128)
            acc = cb[:, sl] + cw[0:1, sl] * u_ref[slot, j, s, HALO - 1 + r0:HALO - 1 + r0 + n, :]
            for k in range(1, FFN_CONV):
                acc = acc + cw[k:k + 1, sl] * u_ref[slot, j, s, HALO - 1 + k + r0:HALO - 1 + k + r0 + n, :]
            outs.append(acc)
        return jnp.concatenate(outs, axis=1)

    def down_rows(c, slot, r0, n):
        val = conv(slot, 0, cw_ref[c], cb_ref[c], r0, n)
        gate = conv(slot, 1, cw_ref[N_FF_CHUNKS + c], cb_ref[N_FF_CHUNKS + c], r0, n)
        return _dot((_silu(gate) * val).astype(BF16), wdn_ref[c])

    for k in range(n_parts):
        r0 = k * part + (HALO if k else 0)
        r1 = (k + 1) * part + (2 * HALO if k == n_parts - 1 else HALO)
        if k == 0:
            h_ref[0:HALO, :] = (rms(xp_ref[...]) * prev_ok).astype(BF16)
        h_ref[HALO + k * part:HALO + (k + 1) * part, :] = rms(x_ref[k * part:(k + 1) * part, :]).astype(BF16)
        if k == n_parts - 1:
            h_ref[HALO + tm:2 * HALO + tm, :] = (rms(xn_ref[...]) * next_ok).astype(BF16)
        up_rows(0, 0, r0, r1)

    assert N_FF_CHUNKS % 2 == 1
    up_stage(1, 1)
    acc_ref[...] = down_rows(0, 0, 0, tm)

    def body(t, carry):
        c = 2 * t + 1
        up_stage(c + 1, 0)
        acc_ref[...] += down_rows(c, 1, 0, tm)
        up_stage(c + 2, 1)
        acc_ref[...] += down_rows(c + 1, 0, 0, tm)
        return carry

    lax.fori_loop(0, N_FF_CHUNKS // 2 - 1, body, 0)
    up_stage(N_FF_CHUNKS - 1, 0)
    acc_ref[...] += down_rows(N_FF_CHUNKS - 2, 1, 0, tm)
    for k in range(n_parts):
        rows = slice(k * part, (k + 1) * part)
        y = x_ref[rows, :] + acc_ref[rows, :] + down_rows(N_FF_CHUNKS - 1, 0, k * part, part)
        if final:
            y = y * lax.rsqrt(jnp.mean(y * y, axis=-1, keepdims=True) + EPS) * gf_ref[...]
        out_ref[rows, :] = y


def _ffn_call(x, lw, norm_final, segs, tm, final):
    n_tok = x.shape[0]
    prev_spec, next_spec = _halo_specs(tm, n_tok)
    tok = pl.BlockSpec((tm, D_MODEL), lambda i: (i, 0))
    return pl.pallas_call(
        functools.partial(_ffn_kernel, segs, tm, final),
        grid=(n_tok // tm,),
        in_specs=[tok, prev_spec, next_spec, _const_spec((1, D_MODEL)),
                  _const_spec((2 * N_FF_CHUNKS, D_MODEL, FF_CHUNK)),
                  _const_spec((2 * N_FF_CHUNKS, FFN_CONV, FF_CHUNK)),
                  _const_spec((2 * N_FF_CHUNKS, 1, FF_CHUNK)),
                  _const_spec((N_FF_CHUNKS, FF_CHUNK, D_MODEL)),
                  _const_spec((1, D_MODEL))],
        out_specs=tok,
        out_shape=jax.ShapeDtypeStruct((n_tok, D_MODEL), F32),
        scratch_shapes=[pltpu.VMEM((tm + 2 * HALO, D_MODEL), BF16), pltpu.VMEM((tm, D_MODEL), F32),
                        pltpu.VMEM((2, 2, FF_CHUNK // 128, tm + 2 * HALO, 128), F32)],
        compiler_params=_params(("arbitrary",)),
        name="ffn",
    )(x, x, x, lw["norm_ffn"], lw["w_up"], lw["ffn_conv_w"], lw["ffn_conv_b"], lw["w_down"], norm_final)


def _layer_weights(i, norm_mix, w_in, gate_bias, ssd_conv_w, ssd_conv_b, ssd_dt_bias, ssd_a_log, ssd_d, ssd_norm,
                   ret_theta, w_branch, w_out, norm_ffn, ffn_w_up, ffn_conv_w, ffn_conv_b, ffn_w_down):
    w = w_in[i]
    offs = np.cumsum([0, SSD_WIDTH, SSD_XBC, 2 * SSD_HEADS, RET_IN_W, NA_IN_W, GATE_W])
    col = lambda k: w[:, offs[k]:offs[k + 1]]
    up = ffn_w_up[i].astype(BF16).reshape(D_MODEL, 2 * N_FF_CHUNKS, FF_CHUNK).transpose(1, 0, 2)
    return {
        "norm_mix": norm_mix[i].reshape(1, D_MODEL),
        "w_z": col(0).astype(BF16),
        "w_xbc": col(1).astype(BF16),
        "w_dt": col(2).T.astype(BF16),
        "w_ret": col(3).astype(BF16),
        "w_na": col(4).astype(BF16),
        "w_gate": col(5).astype(BF16),
        "gate_bias": gate_bias[i].reshape(1, GATE_W),
        "conv_w": ssd_conv_w[i],
        "conv_b": ssd_conv_b[i].reshape(1, SSD_XBC),
        "dt_bias": ssd_dt_bias[i].reshape(2 * SSD_HEADS, 1),
        "a_neg": -jnp.exp(ssd_a_log[i].astype(F32)).reshape(2 * SSD_HEADS, 1),
        "d_skip": jnp.repeat(ssd_d[i], SSD_HEADDIM).reshape(1, SSD_WIDTH),
        "ssd_norm": ssd_norm[i].reshape(1, SSD_WIDTH),
        "theta_q": jnp.repeat(ret_theta[i], RET_QK, axis=1),
        "theta_v": jnp.repeat(ret_theta[i], RET_V, axis=1),
        "w_branch": w_branch[i].astype(BF16),
        "w_out": w_out[i].astype(BF16),
        "norm_ffn": norm_ffn[i].reshape(1, D_MODEL),
        "w_up": up,
        "ffn_conv_w": ffn_conv_w[i].reshape(FFN_CONV, 2 * N_FF_CHUNKS, FF_CHUNK).transpose(1, 0, 2),
        "ffn_conv_b": ffn_conv_b[i].reshape(2 * N_FF_CHUNKS, 1, FF_CHUNK),
        "w_down": ffn_w_down[i].astype(BF16).reshape(N_FF_CHUNKS, FF_CHUNK, D_MODEL),
    }


def _rope_table(max_len):
    half = RET_QK // 2
    inv = 1.0 / (ROPE_BASE ** (jnp.arange(half, dtype=F32) / half))
    ang = jnp.arange(max_len, dtype=F32)[:, None] * inv[None, :]
    cos = jnp.tile(jnp.cos(ang), (1, 128 // half))
    sin = jnp.tile(jnp.concatenate([-jnp.sin(ang), jnp.sin(ang)], axis=1), (1, 128 // RET_QK))
    return jnp.concatenate([cos, sin], axis=1)


def _pick_tile(segs, tm):
    while any(seq_len % tm for _, seq_len in segs):
        tm //= 2
    assert tm >= CHUNK
    return tm


def _trunk(xs, norm_mix, w_in, gate_bias, ssd_conv_w, ssd_conv_b, ssd_dt_bias, ssd_a_log, ssd_d, ssd_norm,
           ret_theta, na_rpb, w_branch, w_out, norm_ffn, ffn_w_up, ffn_conv_w, ffn_conv_b, ffn_w_down, norm_final):
    segs = tuple((int(x.shape[0]), int(x.shape[1])) for x in xs)
    tm = _pick_tile(segs, PROJ_TILE)
    tm_ffn = _pick_tile(segs, FFN_TILE)
    blk = _mixer_block(segs)
    x = jnp.concatenate([x.reshape(-1, D_MODEL) for x in xs], axis=0)
    rope = _rope_table(max(seq_len for _, seq_len in segs))
    depth = w_in.shape[0]
    gf = norm_final.reshape(1, D_MODEL)
    for i in range(depth):
        lw = _layer_weights(i, norm_mix, w_in, gate_bias, ssd_conv_w, ssd_conv_b, ssd_dt_bias, ssd_a_log, ssd_d,
                            ssd_norm, ret_theta, w_branch, w_out, norm_ffn, ffn_w_up, ffn_conv_w, ffn_conv_b,
                            ffn_w_down)
        gates, ssd_in, dt_raw, ret_in, na_in = _proj_call(x, lw, rope, segs, tm)
        o_ssd = _ssd_call(ssd_in, dt_raw, lw, segs, blk)
        o_ret = _ret_call(ret_in, lw, segs, blk)
        o_na = _na_call(na_in, _na_bias_table(na_rpb[i]), segs, blk)
        x = _merge_call(x, gates, o_ssd, o_ret, o_na, lw, tm_ffn)
        x = _ffn_call(x, lw, gf, segs, tm_ffn, final=(i == depth - 1))
    outs = []
    base = 0
    for (n_seq, seq_len), x_in in zip(segs, xs):
        outs.append(x[base:base + n_seq * seq_len].reshape(x_in.shape))
        base += n_seq * seq_len
    return tuple(outs)


def kernel(x_prompt, x_sample, norm_mix, w_in, gate_bias, ssd_conv_w, ssd_conv_b, ssd_dt_bias, ssd_a_log, ssd_d,
           ssd_norm, ret_theta, na_rpb, w_branch, w_out, norm_ffn, ffn_w_up, ffn_conv_w, ffn_conv_b, ffn_w_down,
           norm_final):
    return _trunk((x_prompt, x_sample), norm_mix, w_in, gate_bias, ssd_conv_w, ssd_conv_b, ssd_dt_bias, ssd_a_log,
                  ssd_d, ssd_norm, ret_theta, na_rpb, w_branch, w_out, norm_ffn, ffn_w_up, ffn_conv_w, ffn_conv_b,
                  ffn_w_down, norm_final)
```

```python
import functools

import jax
import jax.numpy as jnp
import numpy as np
from jax import lax
from jax.experimental import pallas as pl
from jax.experimental.pallas import tpu as pltpu

F32 = jnp.float32
BF16 = jnp.bfloat16

D_MODEL = 1024
GRID_W = 64
CHUNK = 128
SSD_HEADS = 8
SSD_HEADDIM = 64
SSD_WIDTH = SSD_HEADS * SSD_HEADDIM
SSD_GROUPS = 2
SSD_STATE = 128
SSD_BC = SSD_GROUPS * SSD_STATE
SSD_XBC = SSD_WIDTH + 2 * SSD_BC
SSD_CONV = 5
RET_HEADS = 4
RET_QK = 64
RET_V = 128
RET_WIDTH = RET_HEADS * RET_V
ROPE_BASE = 10000.0
NA_HEADS = 8
NA_HEADDIM = 64
NA_WIDTH = NA_HEADS * NA_HEADDIM
NA_ROWS = 8
NA_COLS = 16
N_BRANCH = 3
D_FF = 2816
FFN_CONV = 3
EPS = 1e-6

GATE_W = N_BRANCH * D_MODEL
SSD_IN_W = SSD_WIDTH + SSD_XBC
RET_QK_W = RET_HEADS * RET_QK
RET_IN_W = 2 * RET_QK_W + 2 * RET_WIDTH
NA_IN_W = 3 * NA_WIDTH
NA_WIN = NA_ROWS * GRID_W
HALO = 16
FF_CHUNK = 256
N_FF_CHUNKS = D_FF // FF_CHUNK
NEG_BIG = -1e30
VMEM_LIMIT = 56 * 1024 * 1024
PROJ_TILE = 512
FFN_TILE = 1024
SCAN_UNROLL = 4


def _sigmoid(v):
    return 1.0 / (1.0 + jnp.exp(-v))


def _silu(v):
    return v * _sigmoid(v)


def _dot(a, b):
    return jnp.dot(a, b, preferred_element_type=F32)


def _dot_nt(a, b):
    return lax.dot_general(a, b, (((1,), (1,)), ((), ())), preferred_element_type=F32)


def _dot3(x, m01):
    x1 = x.astype(BF16)
    r1 = x - x1.astype(F32)
    x2 = r1.astype(BF16)
    x3 = (r1 - x2.astype(F32)).astype(BF16)
    return _dot(x1, m01) + _dot(x2, m01) + _dot(x3, m01)


def _tile_pos(i, tm, segs):
    t0 = i * tm
    pos = None
    length = None
    base = 0
    for n_seq, seq_len in segs:
        p = lax.rem(t0 - base, seq_len)
        if pos is None:
            pos, length = p, jnp.int32(seq_len)
        else:
            pos = jnp.where(t0 >= base, p, pos)
            length = jnp.where(t0 >= base, jnp.int32(seq_len), length)
        base += n_seq * seq_len
    return pos, length


def _const_spec(shape):
    nd = len(shape)
    return pl.BlockSpec(shape, lambda *_: (0,) * nd, pipeline_mode=pl.Buffered(1))


def _params(sem):
    return pltpu.CompilerParams(dimension_semantics=sem, vmem_limit_bytes=VMEM_LIMIT)


def _proj_kernel(segs, tm, x_ref, xp_ref, xn_ref, g_ref, wg_ref, gb_ref, wz_ref, wx_ref, cw_ref, cb_ref,
                 wdt_ref, wr_ref, rope_ref, wn_ref, gates_ref, ssd_ref, dt_ref, ret_ref, na_ref, u_ref):
    i = pl.program_id(0)
    pos0, seq_len = _tile_pos(i, tm, segs)
    g = g_ref[...]

    def rms(v):
        return v * lax.rsqrt(jnp.mean(v * v, axis=-1, keepdims=True) + EPS) * g

    h = rms(x_ref[...]).astype(BF16)
    prev_ok = (pos0 > 0).astype(F32)
    next_ok = (pos0 + tm < seq_len).astype(F32)
    hp = (rms(xp_ref[...]) * prev_ok).astype(BF16)
    hn = (rms(xn_ref[...]) * next_ok).astype(BF16)
    hcat = jnp.concatenate([hp, h, hn], axis=0)

    for c in range(GATE_W // 512):
        sl = slice(c * 512, (c + 1) * 512)
        a = _dot(h, wg_ref[:, sl]) + gb_ref[:, sl]
        gates_ref[:, sl] = _sigmoid(a).astype(BF16)

    ssd_ref[:, 0:SSD_WIDTH] = _dot(h, wz_ref[...]).astype(BF16)
    for c in range(SSD_XBC // 512):
        u = _dot(hcat, wx_ref[:, c * 512:(c + 1) * 512])
        for s in range(4):
            u_ref[4 * c + s] = u[:, s * 128:(s + 1) * 128]
    for s in range(SSD_XBC // 128):
        sl = slice(s * 128, (s + 1) * 128)
        acc = cb_ref[:, sl] + cw_ref[0:1, sl] * u_ref[s, HALO - 2:HALO - 2 + tm, :]
        for k in range(1, SSD_CONV):
            acc = acc + cw_ref[k:k + 1, sl] * u_ref[s, HALO - 2 + k:HALO - 2 + k + tm, :]
        ssd_ref[:, SSD_WIDTH + s * 128:SSD_WIDTH + (s + 1) * 128] = _silu(acc).astype(BF16)

    dt_t = _dot_nt(wdt_ref[...], h)
    for k in range(tm // CHUNK):
        dt_ref[k] = dt_t[:, k * CHUNK:(k + 1) * CHUNK]

    cos = rope_ref[:, 0:128]
    sin = rope_ref[:, 128:256]
    lane = lax.broadcasted_iota(jnp.int32, (tm, 128), 1)
    first_half = (lane & (RET_QK - 1)) < RET_QK // 2

    def rotary(t, scale):
        outs = []
        for j in range(RET_QK_W // 128):
            tj = t[:, j * 128:(j + 1) * 128]
            partner = jnp.where(first_half, pltpu.roll(tj, 128 - RET_QK // 2, axis=1),
                                pltpu.roll(tj, RET_QK // 2, axis=1))
            outs.append((tj * cos + partner * sin) * scale)
        return jnp.concatenate(outs, axis=1)

    ret_ref[:, 0:RET_QK_W] = rotary(_dot(h, wr_ref[:, 0:RET_QK_W]), 1.0).astype(BF16)
    ret_ref[:, RET_QK_W:2 * RET_QK_W] = rotary(_dot(h, wr_ref[:, RET_QK_W:2 * RET_QK_W]),
                                              RET_QK ** -0.5).astype(BF16)
    for c in range(2):
        sl = slice(2 * RET_QK_W + c * 512, 2 * RET_QK_W + (c + 1) * 512)
        ret_ref[:, sl] = _dot(h, wr_ref[:, sl]).astype(BF16)

    for c in range(NA_IN_W // 512):
        sl = slice(c * 512, (c + 1) * 512)
        na_ref[:, sl] = _dot(h, wn_ref[:, sl]).astype(BF16)


def _halo_specs(tm, n_tok, tile0=0):
    r = tm // HALO
    last = n_tok // HALO - 1
    prev = pl.BlockSpec((HALO, D_MODEL), lambda i: (jnp.maximum((tile0 + i) * r - 1, 0), 0))
    nxt = pl.BlockSpec((HALO, D_MODEL), lambda i: (jnp.minimum((tile0 + i + 1) * r, last), 0))
    return prev, nxt


def _proj_call(x, lw, rope, segs, tm):
    n_tok = x.shape[0]

    def rope_map(i):
        pos0, _ = _tile_pos(i, tm, segs)
        return (pos0 // tm, 0)

    prev_spec, next_spec = _halo_specs(tm, n_tok)
    tok = lambda w: pl.BlockSpec((tm, w), lambda i: (i, 0))
    return pl.pallas_call(
        functools.partial(_proj_kernel, segs, tm),
        grid=(n_tok // tm,),
        in_specs=[tok(D_MODEL), prev_spec, next_spec,
                  _const_spec((1, D_MODEL)),
                  _const_spec((D_MODEL, GATE_W)), _const_spec((1, GATE_W)),
                  _const_spec((D_MODEL, SSD_WIDTH)),
                  _const_spec((D_MODEL, SSD_XBC)), _const_spec((SSD_CONV, SSD_XBC)), _const_spec((1, SSD_XBC)),
                  _const_spec((2 * SSD_HEADS, D_MODEL)),
                  _const_spec((D_MODEL, RET_IN_W)),
                  pl.BlockSpec((tm, 256), rope_map),
                  _const_spec((D_MODEL, NA_IN_W))],
        out_specs=[tok(GATE_W), tok(SSD_IN_W),
                   pl.BlockSpec((tm // CHUNK, 2 * SSD_HEADS, CHUNK), lambda i: (i, 0, 0)),
                   tok(RET_IN_W), tok(NA_IN_W)],
        out_shape=[jax.ShapeDtypeStruct((n_tok, GATE_W), BF16),
                   jax.ShapeDtypeStruct((n_tok, SSD_IN_W), BF16),
                   jax.ShapeDtypeStruct((n_tok // CHUNK, 2 * SSD_HEADS, CHUNK), F32),
                   jax.ShapeDtypeStruct((n_tok, RET_IN_W), BF16),
                   jax.ShapeDtypeStruct((n_tok, NA_IN_W), BF16)],
        scratch_shapes=[pltpu.VMEM((SSD_XBC // 128, tm + 2 * HALO, 128), F32)],
        compiler_params=_params(("arbitrary",)),
        name="proj",
    )(x, x, x, lw["norm_mix"], lw["w_gate"], lw["gate_bias"], lw["w_z"], lw["w_xbc"], lw["conv_w"],
      lw["conv_b"], lw["w_dt"], lw["w_ret"], rope, lw["w_na"])


def _mixer_block(segs):
    blk = max(seq_len for _, seq_len in segs)
    base = 0
    for n_seq, seq_len in segs:
        assert blk % seq_len == 0 and base % blk == 0 and (n_seq * seq_len) % blk == 0
        assert seq_len % (SCAN_UNROLL * CHUNK) == 0 and seq_len // GRID_W >= NA_ROWS
        base += n_seq * seq_len
    return blk


def _sub_sequences(segs, blk):
    _, seq_len = _tile_pos(pl.program_id(0), blk, segs)
    return seq_len, blk // seq_len


def _mixer_call(kernel_fn, name, inputs, in_specs, out_width, n_tok, blk, scratch):
    return pl.pallas_call(
        kernel_fn,
        grid=(n_tok // blk,),
        in_specs=in_specs,
        out_specs=pl.BlockSpec((blk, out_width), lambda s: (s, 0)),
        out_shape=jax.ShapeDtypeStruct((n_tok, out_width), BF16),
        scratch_shapes=scratch,
        compiler_params=_params(("arbitrary",)),
        name=name,
    )(*inputs)


def _ssd_kernel(segs, blk, in_ref, dt_ref, dtb_ref, a_ref, dsk_ref, ng_ref, out_ref,
                cadj_ref, pt_ref, sf_in_ref, sb_in_ref, sf_state, sb_state, exp_ref):
    q = CHUNK
    nh = SSD_HEADS
    gw = SSD_WIDTH // SSD_GROUPS
    seq_len, n_sub = _sub_sequences(segs, blk)
    n_chunks = seq_len // q
    ii = lax.broadcasted_iota(jnp.int32, (q, q), 0)
    jj = lax.broadcasted_iota(jnp.int32, (q, q), 1)
    causal = jj <= ii
    pre_m = (ii <= jj).astype(BF16)
    suf_m = (ii >= jj).astype(BF16)
    row16 = lax.broadcasted_iota(jnp.int32, (2 * nh, q), 0)
    is_fwd_row = row16 < nh
    lane_g = lax.broadcasted_iota(jnp.int32, (q, gw), 1)

    er = lax.broadcasted_iota(jnp.int32, (q, 4 * SSD_WIDTH), 0)
    ec = lax.broadcasted_iota(jnp.int32, (q, 4 * SSD_WIDTH), 1)
    ecc = ec & (2 * SSD_WIDTH - 1)
    hd = ((ecc >> 9) << 3) + ((ecc & (SSD_WIDTH - 1)) >> 6)
    exp_ref[...] = (er == jnp.where(ec < 2 * SSD_WIDTH, 2 * nh, 4 * nh) + hd).astype(BF16)

    dtb = dtb_ref[...]
    a_neg = a_ref[...]

    def prep(c):
        v = dt_ref[c] + dtb
        dtv = jnp.maximum(v, 0.0) + jnp.log1p(jnp.exp(-jnp.abs(v)))
        la = dtv * a_neg
        pre = _dot3(la, pre_m)
        suf = _dot3(la, suf_m)
        cum = jnp.where(is_fwd_row, pre, suf)
        tot = jnp.where(is_fwd_row[:, 0:1], pre[:, q - 1:q], suf[:, 0:1])
        wgt = jnp.exp(tot - cum) * dtv
        esc = jnp.exp(cum)
        pack = jnp.concatenate([cum, wgt, esc, jnp.zeros((q - 6 * nh, q), F32)], axis=0)
        cadj_ref[c] = cum - jnp.log(dtv)
        pt_ref[c] = pack.T

    def scan_step(c, fwd):
        state_ref, rec_ref = (sf_state, sf_in_ref) if fwd else (sb_state, sb_in_ref)
        w_cols = slice(0, SSD_WIDTH) if fwd else slice(SSD_WIDTH, 2 * SSD_WIDTH)
        e_cols = slice(2 * SSD_WIDTH, 3 * SSD_WIDTH) if fwd else slice(3 * SSD_WIDTH, 4 * SSD_WIDTH)
        rows = in_ref[pl.ds(pl.multiple_of(c * q, q), q), :]
        ptb = pt_ref[c].astype(BF16)
        w = _dot(ptb, exp_ref[:, w_cols])
        edge = _dot(ptb[q - HALO:q] if fwd else ptb[0:HALO], exp_ref[:, e_cols])
        decay_row = edge[HALO - 1:HALO] if fwd else edge[0:1]
        rec_ref[c] = state_ref[...].astype(BF16)
        xw = (rows[:, SSD_WIDTH:2 * SSD_WIDTH].astype(F32) * w).astype(BF16)
        ds = []
        for gidx in range(SSD_GROUPS):
            b_off = 2 * SSD_WIDTH + gidx * SSD_STATE
            b_t = rows[:, b_off:b_off + SSD_STATE].astype(F32).T.astype(BF16)
            ds.append(_dot(b_t, xw[:, gidx * gw:(gidx + 1) * gw]))
        state_ref[...] = state_ref[...] * decay_row + jnp.concatenate(ds, axis=1)

    def outputs(c):
        r0 = pl.multiple_of(c * q, q)
        rows = in_ref[pl.ds(r0, q), :]
        z = rows[:, 0:SSD_WIDTH].astype(F32)
        xs_b = rows[:, SSD_WIDTH:2 * SSD_WIDTH]
        xs = xs_b.astype(F32)
        cadj = cadj_ref[c]
        pt = pt_ref[c]
        e_fb = _dot(pt.astype(BF16), exp_ref[:, 2 * SSD_WIDTH:4 * SSD_WIDTH])
        e_f = e_fb[:, 0:SSD_WIDTH]
        e_b = e_fb[:, SSD_WIDTH:2 * SSD_WIDTH]
        sf_in = sf_in_ref[c]
        sb_in = sb_in_ref[c]
        ys = []
        for gidx in range(SSD_GROUPS):
            c_off = 2 * SSD_WIDTH + SSD_BC + gidx * SSD_STATE
            b_off = 2 * SSD_WIDTH + gidx * SSD_STATE
            cg = rows[:, c_off:c_off + SSD_STATE]
            bg = rows[:, b_off:b_off + SSD_STATE]
            scores = _dot_nt(cg, bg)
            xg = xs_b[:, gidx * gw:(gidx + 1) * gw]
            yg = jnp.zeros((q, gw), F32)
            for r in range(SSD_HEADS // SSD_GROUPS):
                hf = gidx * (SSD_HEADS // SSD_GROUPS) + r
                hb = nh + hf
                seg_f = jnp.broadcast_to(pt[:, hf:hf + 1], (q, q)) - jnp.broadcast_to(cadj[hf:hf + 1, :], (q, q))
                seg_b = jnp.broadcast_to(pt[:, hb:hb + 1], (q, q)) - jnp.broadcast_to(cadj[hb:hb + 1, :], (q, q))
                m = (scores * jnp.exp(jnp.where(causal, seg_f, seg_b))).astype(BF16)
                yr = _dot(m, xg)
                yg = jnp.where((lane_g >> 6) == r, yr, yg)
            sl = slice(gidx * gw, (gidx + 1) * gw)
            y_inter = _dot(cg, sf_in[:, sl]) * e_f[:, sl] + _dot(cg, sb_in[:, sl]) * e_b[:, sl]
            ys.append(yg + y_inter)
        y = jnp.concatenate(ys, axis=1) + dsk_ref[...] * xs
        o = y * _silu(z)
        o = o * lax.rsqrt(jnp.mean(o * o, axis=-1, keepdims=True) + EPS) * ng_ref[...]
        out_ref[pl.ds(r0, q), :] = o.astype(BF16)

    def sub_body(sub, carry):
        c0 = sub * n_chunks

        def prep_body(t, cy):
            for u in range(SCAN_UNROLL):
                prep(c0 + SCAN_UNROLL * t + u)
            return cy

        lax.fori_loop(0, n_chunks // SCAN_UNROLL, prep_body, 0)

        sf_state[...] = jnp.zeros_like(sf_state)
        sb_state[...] = jnp.zeros_like(sb_state)

        def scan_body(t, cy):
            for u in range(2):
                scan_step(c0 + 2 * t + u, True)
                scan_step(c0 + n_chunks - 1 - 2 * t - u, False)
            return cy

        lax.fori_loop(0, n_chunks // 2, scan_body, 0)

        def out_body(t, cy):
            for u in range(SCAN_UNROLL):
                outputs(c0 + SCAN_UNROLL * t + u)
            return cy

        lax.fori_loop(0, n_chunks // SCAN_UNROLL, out_body, 0)
        return carry

    lax.fori_loop(0, n_sub, sub_body, 0)


def _ssd_call(ssd_in, dt_raw, lw, segs, blk):
    n_tok = ssd_in.shape[0]
    nc = blk // CHUNK
    in_specs = [pl.BlockSpec((blk, SSD_IN_W), lambda s: (s, 0)),
                pl.BlockSpec((nc, 2 * SSD_HEADS, CHUNK), lambda s: (s, 0, 0)),
                _const_spec((2 * SSD_HEADS, 1)), _const_spec((2 * SSD_HEADS, 1)),
                _const_spec((1, SSD_WIDTH)), _const_spec((1, SSD_WIDTH))]
    scratch = [pltpu.VMEM((nc, 2 * SSD_HEADS, CHUNK), F32),
               pltpu.VMEM((nc, CHUNK, 128), F32),
               pltpu.VMEM((nc, SSD_STATE, SSD_WIDTH), BF16),
               pltpu.VMEM((nc, SSD_STATE, SSD_WIDTH), BF16),
               pltpu.VMEM((SSD_STATE, SSD_WIDTH), F32),
               pltpu.VMEM((SSD_STATE, SSD_WIDTH), F32),
               pltpu.VMEM((CHUNK, 4 * SSD_WIDTH), BF16)]
    return _mixer_call(functools.partial(_ssd_kernel, segs, blk), "ssd",
                       [ssd_in, dt_raw, lw["dt_bias"], lw["a_neg"], lw["d_skip"], lw["ssd_norm"]],
                       in_specs, SSD_WIDTH, n_tok, blk, scratch)


def _ret_kernel(segs, blk, in_ref, thq_ref, thv_ref, out_ref, sf_in_ref, sb_in_ref, sf_state, sb_state):
    q = CHUNK
    nh = RET_HEADS
    seq_len, n_sub = _sub_sequences(segs, blk)
    n_chunks = seq_len // q
    lgq = -jnp.exp(thq_ref[...])
    lgv = -jnp.exp(thv_ref[...])
    ii = lax.broadcasted_iota(jnp.int32, (q, q), 0)
    jj = lax.broadcasted_iota(jnp.int32, (q, q), 1)
    causal = jj <= ii
    dist = (ii - jj).astype(F32)
    tok = lax.broadcasted_iota(jnp.int32, (q, 1), 0).astype(F32)

    w_f = jnp.exp((q - 1.0 - tok) * lgq[0:1, :])
    w_b = jnp.exp(tok * lgq[1:2, :])
    e_f = jnp.exp((tok + 1.0) * lgv[0:1, :])
    e_b = jnp.exp((q - tok) * lgv[1:2, :])
    decays = []
    dec_f, dec_b = [], []
    for h in range(nh):
        lf = lgv[0:1, h * RET_V:(h + 1) * RET_V]
        lb = lgv[1:2, h * RET_V:(h + 1) * RET_V]
        decays.append(jnp.exp(jnp.where(causal, dist * lf, -dist * lb)))
        dec_f.append(jnp.broadcast_to(jnp.exp(float(q) * lf), (RET_QK, RET_V)))
        dec_b.append(jnp.broadcast_to(jnp.exp(float(q) * lb), (RET_QK, RET_V)))
    dec_f = jnp.concatenate(dec_f, axis=0)
    dec_b = jnp.concatenate(dec_b, axis=0)
    lane_q = lax.broadcasted_iota(jnp.int32, (q, RET_QK_W), 1) >> 6

    def scan_step(c, fwd):
        state_ref, rec_ref, wrow, dec = (sf_state, sf_in_ref, w_f, dec_f) if fwd else (sb_state, sb_in_ref, w_b, dec_b)
        rows = in_ref[pl.ds(pl.multiple_of(c * q, q), q), :]
        k = rows[:, RET_QK_W:2 * RET_QK_W]
        v = rows[:, 2 * RET_QK_W:2 * RET_QK_W + RET_WIDTH]
        rec_ref[c] = state_ref[...].astype(BF16)
        kw = (k.astype(F32) * wrow).T.astype(BF16)
        full = _dot(kw, v)
        ds = jnp.concatenate([full[h * RET_QK:(h + 1) * RET_QK, h * RET_V:(h + 1) * RET_V] for h in range(nh)],
                             axis=0)
        state_ref[...] = state_ref[...] * dec + ds

    def outputs(c):
        r0 = pl.multiple_of(c * q, q)
        rows = in_ref[pl.ds(r0, q), :]
        qv = rows[:, 0:RET_QK_W]
        kv = rows[:, RET_QK_W:2 * RET_QK_W]
        vv = rows[:, 2 * RET_QK_W:2 * RET_QK_W + RET_WIDTH]
        gv = rows[:, 2 * RET_QK_W + RET_WIDTH:RET_IN_W].astype(F32)
        zero = jnp.zeros_like(qv)
        k_bd = jnp.concatenate([jnp.where(lane_q == h, kv, zero) for h in range(nh)], axis=0)
        s_all = _dot_nt(qv, k_bd)
        sf_in = sf_in_ref[c]
        sb_in = sb_in_ref[c]
        outs = []
        for h in range(nh):
            sl = slice(h * RET_V, (h + 1) * RET_V)
            p = (s_all[:, h * q:(h + 1) * q] * decays[h]).astype(BF16)
            qh = jnp.where(lane_q == h, qv, zero)
            y = (_dot(p, vv[:, sl]) + _dot(qh, sf_in) * e_f[:, sl] + _dot(qh, sb_in) * e_b[:, sl])
            outs.append(y * lax.rsqrt(jnp.mean(y * y, axis=-1, keepdims=True) + EPS))
        o = jnp.concatenate(outs, axis=1) * _silu(gv)
        out_ref[pl.ds(r0, q), :] = o.astype(BF16)

    def sub_body(sub, carry):
        c0 = sub * n_chunks
        sf_state[...] = jnp.zeros_like(sf_state)
        sb_state[...] = jnp.zeros_like(sb_state)

        def scan_body(t, cy):
            scan_step(c0 + t, True)
            scan_step(c0 + n_chunks - 1 - t, False)
            return cy

        lax.fori_loop(0, n_chunks, scan_body, 0)

        def out_body(t, cy):
            for u in range(SCAN_UNROLL):
                outputs(c0 + SCAN_UNROLL * t + u)
            return cy

        lax.fori_loop(0, n_chunks // SCAN_UNROLL, out_body, 0)
        return carry

    lax.fori_loop(0, n_sub, sub_body, 0)


def _ret_call(ret_in, lw, segs, blk):
    n_tok = ret_in.shape[0]
    nc = blk // CHUNK
    in_specs = [pl.BlockSpec((blk, RET_IN_W), lambda s: (s, 0)),
                _const_spec((2, RET_QK_W)), _const_spec((2, RET_WIDTH))]
    scratch = [pltpu.VMEM((nc, RET_QK_W, RET_V), BF16),
               pltpu.VMEM((nc, RET_QK_W, RET_V), BF16),
               pltpu.VMEM((RET_QK_W, RET_V), F32),
               pltpu.VMEM((RET_QK_W, RET_V), F32)]
    return _mixer_call(functools.partial(_ret_kernel, segs, blk), "ret",
                       [ret_in, lw["theta_q"], lw["theta_v"]], in_specs, RET_WIDTH, n_tok, blk, scratch)


def _na_kernel(segs, blk, in_ref, bias_ref, out_ref, s_ref):
    w = GRID_W
    n_pairs = NA_HEADS // 2
    seq_len, n_sub = _sub_sequences(segs, blk)
    n_rows = seq_len // w
    lane = lax.broadcasted_iota(jnp.int32, (w, 128), 1)
    low = lane < NA_HEADDIM
    scale = NA_HEADDIM ** -0.5

    def sub_body(sub, carry):
        row0 = sub * n_rows

        def window(r):
            rs = jnp.clip(r - NA_ROWS // 2, 0, n_rows - NA_ROWS)
            return r - rs, pl.multiple_of((row0 + r) * w, w), pl.multiple_of((row0 + rs) * w, w)

        def logits_stage(r, slot):
            d, q0, k0 = window(r)
            for p in range(n_pairs):
                qp = in_ref[pl.ds(q0, w), p * 128:(p + 1) * 128] * scale
                zero = jnp.zeros_like(qp)
                lhs = jnp.concatenate([jnp.where(low, qp, zero), jnp.where(low, zero, qp)], axis=0)
                kp = in_ref[pl.ds(k0, NA_WIN), NA_WIDTH + p * 128:NA_WIDTH + (p + 1) * 128]
                s_ref[slot, p] = _dot_nt(lhs, kp) + bias_ref[d, p]

        def value_stage(r, slot):
            _, q0, k0 = window(r)
            for p in range(n_pairs):
                s = s_ref[slot, p]
                m = jnp.max(s, axis=-1, keepdims=True)
                e = jnp.exp(s - m)
                l = jnp.sum(e, axis=-1, keepdims=True)
                vp = in_ref[pl.ds(k0, NA_WIN), 2 * NA_WIDTH + p * 128:2 * NA_WIDTH + (p + 1) * 128]
                o = _dot(e.astype(BF16), vp) / l
                out_ref[pl.ds(q0, w), p * 128:(p + 1) * 128] = jnp.where(low, o[0:w], o[w:2 * w]).astype(BF16)

        logits_stage(jnp.int32(0), 0)

        def row_body(t, cy):
            r = 2 * t
            logits_stage(r + 1, 1)
            value_stage(r, 0)
            logits_stage(r + 2, 0)
            value_stage(r + 1, 1)
            return cy

        lax.fori_loop(0, n_rows // 2 - 1, row_body, 0)
        logits_stage(n_rows - 1, 1)
        value_stage(n_rows - 2, 0)
        value_stage(n_rows - 1, 1)
        return carry

    lax.fori_loop(0, n_sub, sub_body, 0)


def _na_call(na_in, bias, segs, blk):
    n_tok = na_in.shape[0]
    in_specs = [pl.BlockSpec((blk, NA_IN_W), lambda s: (s, 0)),
                _const_spec((NA_ROWS, NA_HEADS // 2, 2 * GRID_W, NA_WIN))]
    scratch = [pltpu.VMEM((2, NA_HEADS // 2, 2 * GRID_W, NA_WIN), F32)]
    return _mixer_call(functools.partial(_na_kernel, segs, blk), "na", [na_in, bias],
                       in_specs, NA_WIDTH, n_tok, blk, scratch)


def _na_bias_table(rpb):
    cols = np.arange(GRID_W)
    col_start = np.clip(cols - NA_COLS // 2, 0, GRID_W - NA_COLS)
    kc = np.arange(GRID_W)
    valid = (kc[None, :] >= col_start[:, None]) & (kc[None, :] < col_start[:, None] + NA_COLS)
    col_off = np.clip(kc[None, :] - cols[:, None] + NA_COLS - 1, 0, 2 * NA_COLS - 2)
    toep = jnp.where(valid[None, None], rpb.astype(F32)[:, :, col_off], NEG_BIG)
    per_shift = []
    for d in range(NA_ROWS):
        rows = [toep[:, x - d + NA_ROWS - 1] for x in range(NA_ROWS)]
        per_shift.append(jnp.stack(rows, axis=2).reshape(NA_HEADS, GRID_W, NA_WIN))
    table = jnp.stack(per_shift, axis=0)
    return table.reshape(NA_ROWS, NA_HEADS // 2, 2 * GRID_W, NA_WIN)


def _merge_kernel(x_ref, gates_ref, o0_ref, o1_ref, o2_ref, wb_ref, wo_ref, out_ref):
    merged = None
    for b, o_ref in enumerate((o0_ref, o1_ref, o2_ref)):
        t = gates_ref[:, b * D_MODEL:(b + 1) * D_MODEL].astype(F32) * _dot(o_ref[...], wb_ref[b])
        merged = t if merged is None else merged + t
    out_ref[...] = x_ref[...] + _dot(merged.astype(BF16), wo_ref[...])


def _merge_call(x, gates, o_ssd, o_ret, o_na, lw, tm):
    n_tok = x.shape[0]
    tok = lambda w: pl.BlockSpec((tm, w), lambda i: (i, 0))
    return pl.pallas_call(
        _merge_kernel,
        grid=(n_tok // tm,),
        in_specs=[tok(D_MODEL), tok(GATE_W), tok(SSD_WIDTH), tok(RET_WIDTH), tok(NA_WIDTH),
                  _const_spec((N_BRANCH, SSD_WIDTH, D_MODEL)), _const_spec((D_MODEL, D_MODEL))],
        out_specs=tok(D_MODEL),
        out_shape=jax.ShapeDtypeStruct((n_tok, D_MODEL), F32),
        compiler_params=_params(("arbitrary",)),
        name="merge",
    )(x, gates, o_ssd, o_ret, o_na, lw["w_branch"], lw["w_out"])


def _ffn_kernel(segs, tm, tile0, final, x_ref, xp_ref, xn_ref, g_ref, wup_ref, cw_ref, cb_ref, wdn_ref, gf_ref,
                out_ref, h_ref, acc_ref, u_ref):
    i = tile0 + pl.program_id(0)
    pos0, seq_len = _tile_pos(i, tm, segs)
    g = g_ref[...]

    def rms(v):
        return v * lax.rsqrt(jnp.mean(v * v, axis=-1, keepdims=True) + EPS) * g

    prev_ok = (pos0 > 0).astype(F32)
    next_ok = (pos0 + tm < seq_len).astype(F32)
    n_slabs = FF_CHUNK // 128
    n_parts = 4
    part = tm // n_parts
    assert part % HALO == 0

    def up_rows(c, slot, r0, r1):
        hs = h_ref[r0:r1, :]
        for j, cc in enumerate((c, N_FF_CHUNKS + c)):
            u = _dot(hs, wup_ref[cc])
            for s in range(n_slabs):
                u_ref[slot, j, s, r0:r1, :] = u[:, s * 128:(s + 1) * 128]

    def up_stage(c, slot):
        up_rows(c, slot, 0, tm + 2 * HALO)

    def conv(slot, j, cw, cb, r0, n):
        outs = []
        for s in range(n_slabs):
            sl = slice(s * 128, (s + 1) * 128)
            acc = cb[:, sl] + cw[0:1, sl] * u_ref[slot, j, s, HALO - 1 + r0:HALO - 1 + r0 + n, :]
            for k in range(1, FFN_CONV):
                acc = acc + cw[k:k + 1, sl] * u_ref[slot, j, s, HALO - 1 + k + r0:HALO - 1 + k + r0 + n, :]
            outs.append(acc)
        return jnp.concatenate(outs, axis=1)

    def down_rows(c, slot, r0, n):
        val = conv(slot, 0, cw_ref[c], cb_ref[c], r0, n)
        gate = conv(slot, 1, cw_ref[N_FF_CHUNKS + c], cb_ref[N_FF_CHUNKS + c], r0, n)
        return _dot((_silu(gate) * val).astype(BF16), wdn_ref[c])

    for k in range(n_parts):
        r0 = k * part + (HALO if k else 0)
        r1 = (k + 1) * part + (2 * HALO if k == n_parts - 1 else HALO)
        if k == 0:
            h_ref[0:HALO, :] = (rms(xp_ref[...]) * prev_ok).astype(BF16)
        h_ref[HALO + k * part:HALO + (k + 1) * part, :] = rms(x_ref[k * part:(k + 1) * part, :]).astype(BF16)
        if k == n_parts - 1:
            h_ref[HALO + tm:2 * HALO + tm, :] = (rms(xn_ref[...]) * next_ok).astype(BF16)
        up_rows(0, 0, r0, r1)

    assert N_FF_CHUNKS % 2 == 1
    up_stage(1, 1)
    acc_ref[...] = down_rows(0, 0, 0, tm)

    def body(t, carry):
        c = 2 * t + 1
        up_stage(c + 1, 0)
        acc_ref[...] += down_rows(c, 1, 0, tm)
        up_stage(c + 2, 1)
        acc_ref[...] += down_rows(c + 1, 0, 0, tm)
        return carry

    lax.fori_loop(0, N_FF_CHUNKS // 2 - 1, body, 0)
    up_stage(N_FF_CHUNKS - 1, 0)
    acc_ref[...] += down_rows(N_FF_CHUNKS - 2, 1, 0, tm)
    for k in range(n_parts):
        rows = slice(k * part, (k + 1) * part)
        y = x_ref[rows, :] + acc_ref[rows, :] + down_rows(N_FF_CHUNKS - 1, 0, k * part, part)
        if final:
            y = y * lax.rsqrt(jnp.mean(y * y, axis=-1, keepdims=True) + EPS) * gf_ref[...]
        out_ref[rows, :] = y


def _ffn_call(x, lw, norm_final, segs, tm, final, tile0=0, n_tiles=None):
    n_tok = x.shape[0]
    n_tiles = n_tok // tm if n_tiles is None else n_tiles
    prev_spec, next_spec = _halo_specs(tm, n_tok, tile0)
    tok = pl.BlockSpec((tm, D_MODEL), lambda i: (i, 0))
    return pl.pallas_call(
        functools.partial(_ffn_kernel, segs, tm, tile0, final),
        grid=(n_tiles,),
        in_specs=[pl.BlockSpec((tm, D_MODEL), lambda i: (tile0 + i, 0)), prev_spec, next_spec,
                  _const_spec((1, D_MODEL)),
                  _const_spec((2 * N_FF_CHUNKS, D_MODEL, FF_CHUNK)),
                  _const_spec((2 * N_FF_CHUNKS, FFN_CONV, FF_CHUNK)),
                  _const_spec((2 * N_FF_CHUNKS, 1, FF_CHUNK)),
                  _const_spec((N_FF_CHUNKS, FF_CHUNK, D_MODEL)),
                  _const_spec((1, D_MODEL))],
        out_specs=tok,
        out_shape=jax.ShapeDtypeStruct((n_tiles * tm, D_MODEL), F32),
        scratch_shapes=[pltpu.VMEM((tm + 2 * HALO, D_MODEL), BF16), pltpu.VMEM((tm, D_MODEL), F32),
                        pltpu.VMEM((2, 2, FF_CHUNK // 128, tm + 2 * HALO, 128), F32)],
        compiler_params=_params(("arbitrary",)),
        name="ffn",
    )(x, x, x, lw["norm_ffn"], lw["w_up"], lw["ffn_conv_w"], lw["ffn_conv_b"], lw["w_down"], norm_final)


def _layer_weights(i, norm_mix, w_in, gate_bias, ssd_conv_w, ssd_conv_b, ssd_dt_bias, ssd_a_log, ssd_d, ssd_norm,
                   ret_theta, w_branch, w_out, norm_ffn, ffn_w_up, ffn_conv_w, ffn_conv_b, ffn_w_down):
    w = w_in[i]
    offs = np.cumsum([0, SSD_WIDTH, SSD_XBC, 2 * SSD_HEADS, RET_IN_W, NA_IN_W, GATE_W])
    col = lambda k: w[:, offs[k]:offs[k + 1]]
    up = ffn_w_up[i].astype(BF16).reshape(D_MODEL, 2 * N_FF_CHUNKS, FF_CHUNK).transpose(1, 0, 2)
    return {
        "norm_mix": norm_mix[i].reshape(1, D_MODEL),
        "w_z": col(0).astype(BF16),
        "w_xbc": col(1).astype(BF16),
        "w_dt": col(2).T.astype(BF16),
        "w_ret": col(3).astype(BF16),
        "w_na": col(4).astype(BF16),
        "w_gate": col(5).astype(BF16),
        "gate_bias": gate_bias[i].reshape(1, GATE_W),
        "conv_w": ssd_conv_w[i],
        "conv_b": ssd_conv_b[i].reshape(1, SSD_XBC),
        "dt_bias": ssd_dt_bias[i].reshape(2 * SSD_HEADS, 1),
        "a_neg": -jnp.exp(ssd_a_log[i].astype(F32)).reshape(2 * SSD_HEADS, 1),
        "d_skip": jnp.repeat(ssd_d[i], SSD_HEADDIM).reshape(1, SSD_WIDTH),
        "ssd_norm": ssd_norm[i].reshape(1, SSD_WIDTH),
        "theta_q": jnp.repeat(ret_theta[i], RET_QK, axis=1),
        "theta_v": jnp.repeat(ret_theta[i], RET_V, axis=1),
        "w_branch": w_branch[i].astype(BF16),
        "w_out": w_out[i].astype(BF16),
        "norm_ffn": norm_ffn[i].reshape(1, D_MODEL),
        "w_up": up,
        "ffn_conv_w": ffn_conv_w[i].reshape(FFN_CONV, 2 * N_FF_CHUNKS, FF_CHUNK).transpose(1, 0, 2),
        "ffn_conv_b": ffn_conv_b[i].reshape(2 * N_FF_CHUNKS, 1, FF_CHUNK),
        "w_down": ffn_w_down[i].astype(BF16).reshape(N_FF_CHUNKS, FF_CHUNK, D_MODEL),
    }


def _rope_table(max_len):
    half = RET_QK // 2
    inv = 1.0 / (ROPE_BASE ** (jnp.arange(half, dtype=F32) / half))
    ang = jnp.arange(max_len, dtype=F32)[:, None] * inv[None, :]
    cos = jnp.tile(jnp.cos(ang), (1, 128 // half))
    sin = jnp.tile(jnp.concatenate([-jnp.sin(ang), jnp.sin(ang)], axis=1), (1, 128 // RET_QK))
    return jnp.concatenate([cos, sin], axis=1)


def _pick_tile(segs, tm):
    while any(seq_len % tm for _, seq_len in segs):
        tm //= 2
    assert tm >= CHUNK
    return tm


def _trunk(xs, norm_mix, w_in, gate_bias, ssd_conv_w, ssd_conv_b, ssd_dt_bias, ssd_a_log, ssd_d, ssd_norm,
           ret_theta, na_rpb, w_branch, w_out, norm_ffn, ffn_w_up, ffn_conv_w, ffn_conv_b, ffn_w_down, norm_final):
    segs = tuple((int(x.shape[0]), int(x.shape[1])) for x in xs)
    tm = _pick_tile(segs, PROJ_TILE)
    tm_ffn = _pick_tile(segs, FFN_TILE)
    blk = _mixer_block(segs)
    x = jnp.concatenate([x.reshape(-1, D_MODEL) for x in xs], axis=0)
    rope = _rope_table(max(seq_len for _, seq_len in segs))
    depth = w_in.shape[0]
    gf = norm_final.reshape(1, D_MODEL)
    for i in range(depth):
        lw = _layer_weights(i, norm_mix, w_in, gate_bias, ssd_conv_w, ssd_conv_b, ssd_dt_bias, ssd_a_log, ssd_d,
                            ssd_norm, ret_theta, w_branch, w_out, norm_ffn, ffn_w_up, ffn_conv_w, ffn_conv_b,
                            ffn_w_down)
        gates, ssd_in, dt_raw, ret_in, na_in = _proj_call(x, lw, rope, segs, tm)
        o_ssd = _ssd_call(ssd_in, dt_raw, lw, segs, blk)
        o_ret = _ret_call(ret_in, lw, segs, blk)
        o_na = _na_call(na_in, _na_bias_table(na_rpb[i]), segs, blk)
        x = _merge_call(x, gates, o_ssd, o_ret, o_na, lw, tm_ffn)
        if i < depth - 1:
            x = _ffn_call(x, lw, gf, segs, tm_ffn, final=False)
    outs = []
    base = 0
    for (n_seq, seq_len), x_in in zip(segs, xs):
        y = _ffn_call(x, lw, gf, segs, tm_ffn, final=True, tile0=base // tm_ffn,
                      n_tiles=n_seq * seq_len // tm_ffn)
        outs.append(y.reshape(x_in.shape))
        base += n_seq * seq_len
    return tuple(outs)


def kernel(x_prompt, x_sample, norm_mix, w_in, gate_bias, ssd_conv_w, ssd_conv_b, ssd_dt_bias, ssd_a_log, ssd_d,
           ssd_norm, ret_theta, na_rpb, w_branch, w_out, norm_ffn, ffn_w_up, ffn_conv_w, ffn_conv_b, ffn_w_down,
           norm_final):
    return _trunk((x_prompt, x_sample), norm_mix, w_in, gate_bias, ssd_conv_w, ssd_conv_b, ssd_dt_bias, ssd_a_log,
                  ssd_d, ssd_norm, ret_theta, na_rpb, w_branch, w_out, norm_ffn, ffn_w_up, ffn_conv_w, ffn_conv_b,
                  ffn_w_down, norm_final)
```
